```python
import math
import jax, jax.numpy as jnp
from jax import lax
import numpy as np


D_MODEL = 1024
BATCH = 2
SEQ = 16384
DEPTH = 2

HEAD_DIM = 64
GRID_W = 64
QBLOCK = 128
ROPE_THETA = 10000.0
RMS_EPS = 1e-6
A_Q_HEADS = 8
A_KV_HEADS = 2
B_GROUPS = ((128, 1), (512, 4), (2048, 16))
B_HEADS_PER_GROUP = 4
C_HEADS = 16
NA_ROWS = 8
NA_COLS = 16
MLP_HIDDEN = 4 * D_MODEL

N_EVEN = (DEPTH + 1) // 2
N_ODD = DEPTH // 2
A_Q_W = A_Q_HEADS * HEAD_DIM
A_KV_W = A_KV_HEADS * HEAD_DIM
B_W = len(B_GROUPS) * B_HEADS_PER_GROUP * HEAD_DIM
AB_IN = A_Q_W + 2 * A_KV_W + 3 * B_W
AB_OUT = A_Q_W + B_HEADS_PER_GROUP * HEAD_DIM
C_W = C_HEADS * HEAD_DIM
C_IN = 3 * C_W

kernel_name = 'hybrid_axial_gqa_dilated_neighbourhood_encoder'


def rms_norm(x, g):
    xf = x.astype(jnp.float32)
    y = xf * lax.rsqrt(jnp.mean(xf * xf, axis=-1, keepdims=True) + RMS_EPS)
    return (y * g.astype(jnp.float32)).astype(x.dtype)


def rope_cos_sin(pos, dim):
    inv_freq = ROPE_THETA ** (-jnp.arange(0, dim, 2, dtype=jnp.float32) / dim)
    ang = pos.astype(jnp.float32)[:, None] * inv_freq[None, :]
    return jnp.cos(ang), jnp.sin(ang)


def apply_rope(x, cos, sin):
    xf = x.astype(jnp.float32)
    x1, x2 = jnp.split(xf, 2, axis=-1)
    return jnp.concatenate([x1 * cos - x2 * sin, x2 * cos + x1 * sin], axis=-1).astype(x.dtype)


def apply_axial_rope(x, row_cs, col_cs):
    half = x.shape[-1] // 2
    return jnp.concatenate([apply_rope(x[..., :half], *row_cs),
                            apply_rope(x[..., half:], *col_cs)], axis=-1)


def dense_gqa_attention(q, k, v):
    b, hkv, g, s, dh = q.shape
    nb = s // QBLOCK
    scale = 1.0 / math.sqrt(dh)
    qb = jnp.moveaxis(q.reshape(b, hkv, g, nb, QBLOCK, dh), 3, 0)

    def one_block(q_blk):
        sc = jnp.einsum('bkgqd,bksd->bkgqs', q_blk, k, preferred_element_type=jnp.float32) * scale
        p = jax.nn.softmax(sc, axis=-1)
        return jnp.einsum('bkgqs,bksd->bkgqd', p.astype(v.dtype), v)

    o = lax.map(one_block, qb)
    return jnp.moveaxis(o, 0, 3).reshape(b, hkv * g, s, dh)


def gathered_attention(q, k, v, idx, extra, score_mod):
    b, h, s, dh = q.shape
    nk = idx.shape[-1]
    nb = s // QBLOCK
    scale = 1.0 / math.sqrt(dh)
    qb = jnp.moveaxis(q.reshape(b, h, nb, QBLOCK, dh), 2, 0)
    ib = idx.reshape(nb, QBLOCK, nk)
    eb = extra.reshape(nb, QBLOCK, nk)

    def one_block(args):
        q_blk, i_blk, e_blk = args
        kg = jnp.take(k, i_blk, axis=2)
        vg = jnp.take(v, i_blk, axis=2)
        sc = jnp.einsum('bhqd,bhqkd->bhqk', q_blk, kg, preferred_element_type=jnp.float32) * scale
        sc = score_mod(sc, e_blk)
        m = jnp.max(sc, axis=-1, keepdims=True)
        p = jnp.exp(sc - m)
        l = jnp.sum(p, axis=-1, keepdims=True)
        o = jnp.einsum('bhqk,bhqkd->bhqd', (p / l).astype(vg.dtype), vg)
        return o, (m + jnp.log(l))[..., 0]

    o, lse = lax.map(one_block, (qb, ib, eb))
    o = jnp.moveaxis(o, 0, 2).reshape(b, h, s, dh)
    lse = jnp.moveaxis(lse, 0, 2).reshape(b, h, s)
    return o, lse


def dilated_indices(s, window, dilation):
    half = window // (2 * dilation)
    off = dilation * jnp.arange(-half, half + 1, dtype=jnp.int32)
    pos = jnp.arange(s, dtype=jnp.int32)[:, None] + off[None, :]
    valid = (pos >= 0) & (pos < s)
    return jnp.clip(pos, 0, s - 1), valid


def neighbourhood_indices(s):
    rows = s // GRID_W
    wh = min(NA_ROWS, rows)
    t = jnp.arange(s, dtype=jnp.int32)
    r, c = t // GRID_W, t % GRID_W
    rs = jnp.clip(r - wh // 2, 0, rows - wh)
    cs = jnp.clip(c - NA_COLS // 2, 0, GRID_W - NA_COLS)
    kr = rs[:, None] + jnp.arange(wh, dtype=jnp.int32)[None, :]
    kc = cs[:, None] + jnp.arange(NA_COLS, dtype=jnp.int32)[None, :]
    idx = (kr[:, :, None] * GRID_W + kc[:, None, :]).reshape(s, wh * NA_COLS)
    dr = kr - r[:, None] + (NA_ROWS - 1)
    dc = kc - c[:, None] + (NA_COLS - 1)
    bidx = (dr[:, :, None] * (2 * NA_COLS - 1) + dc[:, None, :]).reshape(s, wh * NA_COLS)
    return idx, bidx


def mask_mod(sc, valid):
    return jnp.where(valid[None, None], sc, -jnp.inf)


def mixer_ab(h, w_in, w_out, q_gain, k_gain):
    b, s, _ = h.shape
    proj = h @ w_in
    p1 = A_Q_W
    p2 = p1 + A_KV_W
    p3 = p2 + A_KV_W
    p4 = p3 + B_W
    p5 = p4 + B_W
    qa, ka, va, qb, kb, vb = jnp.split(proj, [p1, p2, p3, p4, p5], axis=-1)
    t = jnp.arange(s, dtype=jnp.int32)

    grp = A_Q_HEADS // A_KV_HEADS
    qa = qa.reshape(b, s, A_KV_HEADS, grp, HEAD_DIM).transpose(0, 2, 3, 1, 4)
    ka = ka.reshape(b, s, A_KV_HEADS, HEAD_DIM).transpose(0, 2, 1, 3)
    va = va.reshape(b, s, A_KV_HEADS, HEAD_DIM).transpose(0, 2, 1, 3)
    row_cs = rope_cos_sin(t // GRID_W, HEAD_DIM // 2)
    col_cs = rope_cos_sin(t % GRID_W, HEAD_DIM // 2)
    qa = apply_axial_rope(rms_norm(qa, q_gain), row_cs, col_cs)
    ka = apply_axial_rope(rms_norm(ka, k_gain), row_cs, col_cs)
    oa = dense_gqa_attention(qa, ka, va)

    n_g = len(B_GROUPS)

    def heads(z):
        return z.reshape(b, s, n_g, B_HEADS_PER_GROUP, HEAD_DIM).transpose(2, 0, 3, 1, 4)

    cs1 = rope_cos_sin(t, HEAD_DIM)
    qb = apply_rope(heads(qb), *cs1)
    kb = apply_rope(heads(kb), *cs1)
    vb = heads(vb)
    outs, lses = [], []
    for gi, (window, dil) in enumerate(B_GROUPS):
        idx, valid = dilated_indices(s, window, dil)
        o_g, lse_g = gathered_attention(qb[gi], kb[gi], vb[gi], idx, valid, mask_mod)
        outs.append(o_g.astype(jnp.float32))
        lses.append(lse_g)
    wts = jax.nn.softmax(jnp.stack(lses, axis=0), axis=0)
    ob = jnp.sum(wts[..., None] * jnp.stack(outs, axis=0), axis=0).astype(h.dtype)

    o = jnp.concatenate([oa, ob], axis=1)
    return o.transpose(0, 2, 1, 3).reshape(b, s, AB_OUT) @ w_out


def mixer_c(h, w_in, w_out, rpb):
    b, s, _ = h.shape
    q, k, v = jnp.split(h @ w_in, 3, axis=-1)
    q, k, v = [z.reshape(b, s, C_HEADS, HEAD_DIM).transpose(0, 2, 1, 3) for z in (q, k, v)]
    idx, bidx = neighbourhood_indices(s)
    table = rpb.reshape(C_HEADS, -1).astype(jnp.float32)
    o, _ = gathered_attention(q, k, v, idx, bidx, lambda sc, e: sc + table[:, e][None])
    return o.transpose(0, 2, 1, 3).reshape(b, s, C_W) @ w_out


def sq_relu_mlp(h, w_up, w_down):
    return jnp.square(jax.nn.relu(h @ w_up)) @ w_down


def sandwich(x, mod, g_pre, g_post, fn):
    shift, scale, gate = jnp.split(mod[:, None, :], 3, axis=-1)
    h = rms_norm(x, g_pre) * (1 + scale) + shift
    return x + gate * rms_norm(fn(h), g_post)


def setup_inputs(seed: int = 0) -> dict:
    key = jax.random.key(seed)
    ks = jax.random.split(key, 14)
    D = D_MODEL

    def normal(k, shape, std):
        return jax.random.normal(k, shape, jnp.float32) * std

    return {
        'x': normal(ks[0], (BATCH, SEQ, D), 1.0),
        'c': normal(ks[1], (BATCH, D), 1.0),
        'ada_w': normal(ks[2], (DEPTH, 2, D, 3 * D), D ** -0.5),
        'ada_b': normal(ks[3], (DEPTH, 2, 3 * D), 0.02),
        'norm_g': 1.0 + normal(ks[4], (DEPTH, 4, D), 0.1),
        'ab_w_in': normal(ks[5], (N_EVEN, D, AB_IN), D ** -0.5),
        'ab_w_out': normal(ks[6], (N_EVEN, AB_OUT, D), AB_OUT ** -0.5),
        'a_q_gain': 1.0 + normal(ks[7], (N_EVEN, HEAD_DIM), 0.1),
        'a_k_gain': 1.0 + normal(ks[8], (N_EVEN, HEAD_DIM), 0.1),
        'c_w_in': normal(ks[9], (N_ODD, D, C_IN), D ** -0.5),
        'c_w_out': normal(ks[10], (N_ODD, C_W, D), C_W ** -0.5),
        'c_rpb': normal(ks[11], (N_ODD, C_HEADS, 2 * NA_ROWS - 1, 2 * NA_COLS - 1), 0.1),
        'mlp_w_up': normal(ks[12], (DEPTH, D, MLP_HIDDEN), D ** -0.5),
        'mlp_w_down': normal(ks[13], (DEPTH, MLP_HIDDEN, D), MLP_HIDDEN ** -0.5),
    }


def reference(x, c, ada_w, ada_b, norm_g, ab_w_in, ab_w_out, a_q_gain, a_k_gain,
              c_w_in, c_w_out, c_rpb, mlp_w_up, mlp_w_down):
    cond = jax.nn.silu(c)
    for layer in range(DEPTH):
        mod_mix = cond @ ada_w[layer, 0] + ada_b[layer, 0]
        mod_mlp = cond @ ada_w[layer, 1] + ada_b[layer, 1]
        i = layer // 2
        if layer % 2 == 0:
            mix = functools_partial_ab(ab_w_in[i], ab_w_out[i], a_q_gain[i], a_k_gain[i])
        else:
            mix = functools_partial_c(c_w_in[i], c_w_out[i], c_rpb[i])
        x = sandwich(x, mod_mix, norm_g[layer, 0], norm_g[layer, 1], mix)
        x = sandwich(x, mod_mlp, norm_g[layer, 2], norm_g[layer, 3],
                     lambda h, wu=mlp_w_up[layer], wd=mlp_w_down[layer]: sq_relu_mlp(h, wu, wd))
    return x


def functools_partial_ab(w_in, w_out, q_gain, k_gain):
    return lambda h: mixer_ab(h, w_in, w_out, q_gain, k_gain)


def functools_partial_c(w_in, w_out, rpb):
    return lambda h: mixer_c(h, w_in, w_out, rpb)
```

```python
import functools
import math

import numpy as np
import jax
import jax.numpy as jnp
from jax import lax
from jax.experimental import pallas as pl
from jax.experimental.pallas import tpu as pltpu

F32 = jnp.float32
BF16 = jnp.bfloat16

HEAD_DIM = 64
GRID_W = 64
ROPE_THETA = 10000.0
RMS_EPS = 1e-6
A_Q_HEADS = 8
A_KV_HEADS = 2
B_GROUPS = ((128, 1), (512, 4), (2048, 16))
B_HEADS_PER_GROUP = 4
C_HEADS = 16
NA_ROWS = 8
NA_COLS = 16
A_Q_W = A_Q_HEADS * HEAD_DIM
A_KV_W = A_KV_HEADS * HEAD_DIM
B_GROUP_W = B_HEADS_PER_GROUP * HEAD_DIM
B_W = len(B_GROUPS) * B_GROUP_W
AB_A_W = A_Q_W + 2 * A_KV_W
QK_SCALE = 1.0 / math.sqrt(HEAD_DIM)

LANES = 128
V7X_VMEM_LIMIT_BYTES = 56 * 1024 * 1024
NEG_BIG = -1e30

TOKEN_TILE = 512
A_TQ = 256
A_TK = 256
A_V_ROWS = 80
B_TQ = 128
C_PAIRS = 8
C_WIN_ROWS = 10


def _cparams(n_axes):
    return pltpu.CompilerParams(dimension_semantics=("arbitrary",) * n_axes,
                                vmem_limit_bytes=V7X_VMEM_LIMIT_BYTES)


def _resident(block_shape, index_map):
    return pl.BlockSpec(block_shape, index_map, pipeline_mode=pl.Buffered(1))


def _rms(x):
    return x * lax.rsqrt(jnp.mean(x * x, axis=-1, keepdims=True) + RMS_EPS)


def _dot(a, b):
    return jnp.dot(a, b, preferred_element_type=F32)


def _dot_nt(a, b):
    return lax.dot_general(a, b, (((1,), (1,)), ((), ())), preferred_element_type=F32)


def _rope_slab(x, cos, sin_signed, half):
    lane = lax.broadcasted_iota(jnp.int32, x.shape, 1)
    first = (lane & half) == 0
    rot = jnp.where(first, pltpu.roll(x, LANES - half, 1), pltpu.roll(x, half, 1))
    return x * cos + rot * sin_signed


def _adaln_kernel(ct_ref, w_ref, b_ref, o_ref):
    ct = ct_ref[...]
    cond = ct * jax.nn.sigmoid(ct)
    w = w_ref[0]
    rows = [jnp.sum(w * cond[:, b:b + 1], axis=0, keepdims=True) for b in range(ct.shape[1])]
    o_ref[0] = jnp.concatenate(rows, axis=0) + b_ref[0]


def _adaln(c, ada_w, ada_b):
    nb, d = c.shape
    n_mod = ada_w.shape[0] * ada_w.shape[1]
    n_out = ada_w.shape[-1]
    tn = 768
    w = ada_w.reshape(n_mod, d, n_out)
    b = ada_b.reshape(n_mod, 1, n_out)
    return pl.pallas_call(
        _adaln_kernel,
        grid=(n_mod, n_out // tn),
        in_specs=[pl.BlockSpec((d, nb), lambda m, j: (0, 0)),
                  pl.BlockSpec((1, d, tn), lambda m, j: (m, 0, j)),
                  pl.BlockSpec((1, 1, tn), lambda m, j: (m, 0, j))],
        out_specs=pl.BlockSpec((1, nb, tn), lambda m, j: (m, 0, j)),
        out_shape=jax.ShapeDtypeStruct((n_mod, nb, n_out), F32),
        compiler_params=_cparams(2),
        name="adaln",
    )(c.T, w, b)


def _proj0_kernel(x_ref, g_ref, sh_ref, sc_ref, w_ref, ca_ref, sa_ref, cb_ref, sb_ref, gain_ref, gmat_ref,
                  qa_ref, ka_ref, vt_ref, qb_ref, kb_ref, vb_ref):
    tm = x_ref.shape[1]
    x = x_ref[0]
    h = _rms(x) * g_ref[...] * (1.0 + sc_ref[0]) + sh_ref[0]
    hb = h.astype(BF16)

    pa = _dot(hb, w_ref[:, 0:AB_A_W])
    n_qk = A_Q_W + A_KV_W
    qk = pa[:, 0:n_qk]
    sq = qk * qk
    hi = sq.astype(BF16)
    lo = (sq - hi.astype(F32)).astype(BF16)
    ms = (_dot(hi, gmat_ref[...]) + _dot(lo, gmat_ref[...])) * (1.0 / HEAD_DIM)
    qk = qk * lax.rsqrt(ms + RMS_EPS) * gain_ref[...]
    ca = ca_ref[...]
    sa = sa_ref[...]
    for j in range(n_qk // LANES):
        slab = _rope_slab(qk[:, j * LANES:(j + 1) * LANES], ca, sa, HEAD_DIM // 4).astype(BF16)
        if j < A_Q_W // LANES:
            qa_ref[0, :, j * LANES:(j + 1) * LANES] = slab
        else:
            ka_ref[0] = slab
    vt = pa[:, n_qk:AB_A_W].T
    ones = jnp.ones((A_V_ROWS - HEAD_DIM, A_TK), BF16)
    for kvh in range(A_KV_HEADS):
        for c in range(tm // A_TK):
            vt_ref[0, kvh, c, 0:HEAD_DIM, :] = vt[kvh * HEAD_DIM:(kvh + 1) * HEAD_DIM,
                                                  c * A_TK:(c + 1) * A_TK].astype(BF16)
            vt_ref[0, kvh, c, HEAD_DIM:A_V_ROWS, :] = ones

    pb = _dot(hb, w_ref[:, AB_A_W:AB_A_W + 3 * B_W])
    cb = cb_ref[...]
    sb = sb_ref[...]
    for j in range(B_W // LANES):
        q = _rope_slab(pb[:, j * LANES:(j + 1) * LANES], cb, sb, HEAD_DIM // 2) * QK_SCALE
        qb_ref[0, :, j * LANES:(j + 1) * LANES] = q.astype(BF16)
        k = _rope_slab(pb[:, B_W + j * LANES:B_W + (j + 1) * LANES], cb, sb, HEAD_DIM // 2)
        kb_ref[0, :, j * LANES:(j + 1) * LANES] = k.astype(BF16)
    vb_ref[0] = pb[:, 2 * B_W:3 * B_W].astype(BF16)


def _proj0(x, g_pre, shift, scale, w_in, ropes, gain_qk, gmat):
    nb, s, d = x.shape
    tm = TOKEN_TILE
    ca, sa, cb, sb = ropes
    tok = lambda b, i: (b, i, 0)
    mod = lambda b, i: (b, 0, 0)
    rope_spec = pl.BlockSpec((tm, LANES), lambda b, i: (i, 0))
    n_qk = A_Q_W + A_KV_W
    out_shape = (
        jax.ShapeDtypeStruct((nb, s, A_Q_W), BF16),
        jax.ShapeDtypeStruct((nb, s, A_KV_W), BF16),
        jax.ShapeDtypeStruct((nb, A_KV_HEADS, s // A_TK, A_V_ROWS, A_TK), BF16),
        jax.ShapeDtypeStruct((nb, s, B_W), BF16),
        jax.ShapeDtypeStruct((nb, s, B_W), BF16),
        jax.ShapeDtypeStruct((nb, s, B_W), BF16),
    )
    out_specs = (
        pl.BlockSpec((1, tm, A_Q_W), tok),
        pl.BlockSpec((1, tm, A_KV_W), tok),
        pl.BlockSpec((1, A_KV_HEADS, tm // A_TK, A_V_ROWS, A_TK), lambda b, i: (b, 0, i, 0, 0)),
        pl.BlockSpec((1, tm, B_W), tok),
        pl.BlockSpec((1, tm, B_W), tok),
        pl.BlockSpec((1, tm, B_W), tok),
    )
    return pl.pallas_call(
        _proj0_kernel,
        grid=(nb, s // tm),
        in_specs=[pl.BlockSpec((1, tm, d), tok),
                  pl.BlockSpec((1, d), lambda b, i: (0, 0)),
                  pl.BlockSpec((1, 1, d), mod),
                  pl.BlockSpec((1, 1, d), mod),
                  _resident(w_in.shape, lambda b, i: (0, 0)),
                  rope_spec, rope_spec, rope_spec, rope_spec,
                  pl.BlockSpec((1, n_qk), lambda b, i: (0, 0)),
                  _resident((n_qk, n_qk), lambda b, i: (0, 0))],
        out_specs=out_specs,
        out_shape=out_shape,
        compiler_params=_cparams(2),
        name="proj0",
    )(x, g_pre, shift, scale, w_in, ca, sa, cb, sb, gain_qk, gmat)


def _attn_a_kernel(q_ref, k_ref, vt_ref, o_ref, qbd_ref, m_ref, acc_ref):
    tq = q_ref.shape[1]
    n_chunks = k_ref.shape[1] // A_TK
    qt = q_ref[0].astype(F32).T
    zeros = jnp.zeros((HEAD_DIM, tq), F32)
    group = A_Q_HEADS // A_KV_HEADS
    for h in range(A_Q_HEADS):
        blk = qt[h * HEAD_DIM:(h + 1) * HEAD_DIM]
        full = jnp.concatenate([blk, zeros] if h < group else [zeros, blk], axis=0)
        qbd_ref[:, h * tq:(h + 1) * tq] = full.astype(BF16)
    m_ref[...] = jnp.full(m_ref.shape, -jnp.inf, F32)
    acc_ref[...] = jnp.zeros(acc_ref.shape, F32)

    def body(c, carry):
        k = k_ref[0, pl.ds(pl.multiple_of(c * A_TK, A_TK), A_TK), :]
        s = _dot(k, qbd_ref[...])
        m_old = m_ref[...]
        m_new = jnp.maximum(m_old, jnp.max(s, axis=0, keepdims=True))
        alpha = jnp.exp(m_old - m_new)
        p = jnp.exp(s - m_new).astype(BF16)
        m_ref[...] = m_new
        for h in range(A_Q_HEADS):
            pv = _dot(vt_ref[0, h // group, c], p[:, h * tq:(h + 1) * tq])
            acc_ref[h] = acc_ref[h] * alpha[:, h * tq:(h + 1) * tq] + pv
        return carry

    lax.fori_loop(0, n_chunks, body, 0)
    outs = []
    for h in range(A_Q_HEADS):
        a = acc_ref[h]
        outs.append(a[0:HEAD_DIM] / a[HEAD_DIM:HEAD_DIM + 1])
    o_ref[0] = jnp.concatenate(outs, axis=0).T.astype(BF16)


def _attn_a(qa, ka, vt):
    nb, s, _ = qa.shape
    tq = A_TQ
    return pl.pallas_call(
        _attn_a_kernel,
        grid=(nb, s // tq),
        in_specs=[pl.BlockSpec((1, tq, A_Q_W), lambda b, i: (b, i, 0)),
                  _resident((1, s, A_KV_W), lambda b, i: (b, 0, 0)),
                  _resident((1,) + vt.shape[1:], lambda b, i: (b, 0, 0, 0, 0))],
        out_specs=pl.BlockSpec((1, tq, A_Q_W), lambda b, i: (b, i, 0)),
        out_shape=jax.ShapeDtypeStruct((nb, s, A_Q_W), BF16),
        scratch_shapes=[pltpu.VMEM((2 * HEAD_DIM, A_Q_HEADS * tq), BF16),
                        pltpu.VMEM((1, A_Q_HEADS * tq), F32),
                        pltpu.VMEM((A_Q_HEADS, A_V_ROWS, tq), F32)],
        compiler_params=_cparams(2),
        name="attn_a",
    )(qa, ka, vt)


def _pair_softmax_pv(q, kw, vw, bias_of_head):
    lane = lax.broadcasted_iota(jnp.int32, q.shape, 1)
    left = lane < HEAD_DIM
    zero = jnp.zeros_like(q)
    out = []
    for hd in range(2):
        qh = jnp.where(left, q, zero) if hd == 0 else jnp.where(left, zero, q)
        s = _dot_nt(qh, kw) + bias_of_head(hd)
        m = jnp.max(s, axis=-1, keepdims=True)
        p = jnp.exp(s - m)
        l = jnp.sum(p, axis=-1, keepdims=True)
        o = _dot(p.astype(BF16), vw) * (1.0 / l)
        out.append((o, m, l))
    return left, out


def _attn_b_kernel(q_ref, k_ref, v_ref, bias_ref, o_ref, lse_ref, *, half_w, win):
    tq = q_ref.shape[1]
    seq = k_ref.shape[1]
    t0 = pl.program_id(2) * tq
    start = pl.multiple_of(jnp.clip(t0 - half_w, 0, seq - win), HEAD_DIM)
    kw = k_ref[0, pl.ds(start, win), :]
    vw = v_ref[0, pl.ds(start, win), :]
    bias = bias_ref[0]
    left, res = _pair_softmax_pv(q_ref[0], kw, vw, lambda hd: bias)
    (o0, m0, l0), (o1, m1, l1) = res
    o_ref[0] = jnp.where(left, o0, o1)
    lse_ref[0] = jnp.where(left, m0 + jnp.log(l0), m1 + jnp.log(l1))


def _dilated_plan(s, window, dilation, tq):
    half_w = window // 2
    win = tq + 2 * half_w
    assert win <= s, "sequence shorter than one dilated key window"
    nblk = s // tq
    n_edge = -(-half_w // tq)
    shifts = [half_w - i * tq for i in range(n_edge)] + [0] + \
             [(s - win) - ((nblk - n_edge + i) * tq - half_w) for i in range(n_edge)]
    sh = jnp.asarray(shifts, jnp.int32)[:, None, None]
    qi = jnp.arange(tq, dtype=jnp.int32)[None, :, None]
    kj = jnp.arange(win, dtype=jnp.int32)[None, None, :]
    delta = kj - qi + sh - half_w
    valid = (jnp.abs(delta) <= half_w) & ((delta & (dilation - 1)) == 0)
    bias = jnp.where(valid, 0.0, NEG_BIG).astype(F32)

    def variant(i):
        return jnp.where(i < n_edge, i, jnp.where(i >= nblk - n_edge, i - (nblk - 2 * n_edge) + 1, n_edge))

    return half_w, win, bias, variant


def _attn_b_group(qb, kb, vb, gi):
    nb, s, _ = qb.shape
    window, dilation = B_GROUPS[gi]
    tq = B_TQ
    half_w, win, bias, variant = _dilated_plan(s, window, dilation, tq)
    pairs = B_GROUP_W // LANES
    col = lambda hp: gi * pairs + hp
    kern = functools.partial(_attn_b_kernel, half_w=half_w, win=win)
    return pl.pallas_call(
        kern,
        grid=(nb, pairs, s // tq),
        in_specs=[pl.BlockSpec((1, tq, LANES), lambda b, hp, i: (b, i, col(hp))),
                  _resident((1, s, LANES), lambda b, hp, i: (b, 0, col(hp))),
                  _resident((1, s, LANES), lambda b, hp, i: (b, 0, col(hp))),
                  pl.BlockSpec((1, tq, win), lambda b, hp, i: (variant(i), 0, 0))],
        out_specs=(pl.BlockSpec((1, tq, LANES), lambda b, hp, i: (b, i, hp)),
                   pl.BlockSpec((1, tq, LANES), lambda b, hp, i: (b, i, hp))),
        out_shape=(jax.ShapeDtypeStruct((nb, s, B_GROUP_W), F32),
                   jax.ShapeDtypeStruct((nb, s, B_GROUP_W), F32)),
        compiler_params=_cparams(3),
        name=f"attn_b{gi}",
    )(qb, kb, vb, bias)


def _finish(y, x, g_post, gate):
    return x + gate * (_rms(y) * g_post)


def _out0_kernel(oa_ref, o0_ref, l0_ref, o1_ref, l1_ref, o2_ref, l2_ref, wa_ref, wb_ref,
                 x_ref, g_ref, gate_ref, y_ref):
    l0, l1, l2 = l0_ref[0], l1_ref[0], l2_ref[0]
    m = jnp.maximum(jnp.maximum(l0, l1), l2)
    e0, e1, e2 = jnp.exp(l0 - m), jnp.exp(l1 - m), jnp.exp(l2 - m)
    ob = (e0 * o0_ref[0] + e1 * o1_ref[0] + e2 * o2_ref[0]) / (e0 + e1 + e2)
    y = _dot(oa_ref[0], wa_ref[...]) + _dot(ob.astype(BF16), wb_ref[...])
    y_ref[0] = _finish(y, x_ref[0], g_ref[...], gate_ref[0])


def _out0(oa, groups, w_out, x, g_post, gate):
    nb, s, d = x.shape
    tm = TOKEN_TILE
    tok = lambda b, i: (b, i, 0)
    grp_spec = pl.BlockSpec((1, tm, B_GROUP_W), tok)
    flat = [t for pair in groups for t in pair]
    wa, wb = w_out[:A_Q_W], w_out[A_Q_W:]
    return pl.pallas_call(
        _out0_kernel,
        grid=(nb, s // tm),
        in_specs=[pl.BlockSpec((1, tm, A_Q_W), tok)] + [grp_spec] * 6 +
                 [_resident(wa.shape, lambda b, i: (0, 0)),
                  _resident(wb.shape, lambda b, i: (0, 0)),
                  pl.BlockSpec((1, tm, d), tok),
                  pl.BlockSpec((1, d), lambda b, i: (0, 0)),
                  pl.BlockSpec((1, 1, d), lambda b, i: (b, 0, 0))],
        out_specs=pl.BlockSpec((1, tm, d), tok),
        out_shape=jax.ShapeDtypeStruct(x.shape, F32),
        compiler_params=_cparams(2),
        name="out0",
    )(oa, *flat, wa, wb, x, g_post, gate)


def _out1_kernel(o_ref, w_ref, x_ref, g_ref, gate_ref, y_ref):
    y_ref[0] = _finish(_dot(o_ref[0], w_ref[...]), x_ref[0], g_ref[...], gate_ref[0])


def _out1(o, w_out, x, g_post, gate):
    nb, s, d = x.shape
    tm = TOKEN_TILE
    tok = lambda b, i: (b, i, 0)
    return pl.pallas_call(
        _out1_kernel,
        grid=(nb, s // tm),
        in_specs=[pl.BlockSpec((1, tm, o.shape[-1]), tok),
                  _resident(w_out.shape, lambda b, i: (0, 0)),
                  pl.BlockSpec((1, tm, d), tok),
                  pl.BlockSpec((1, d), lambda b, i: (0, 0)),
                  pl.BlockSpec((1, 1, d), lambda b, i: (b, 0, 0))],
        out_specs=pl.BlockSpec((1, tm, d), tok),
        out_shape=jax.ShapeDtypeStruct(x.shape, F32),
        compiler_params=_cparams(2),
        name="out1",
    )(o, w_out, x, g_post, gate)


def _mlp_kernel(x_ref, gpre_ref, sh_ref, sc_ref, wu_ref, wd_ref, gpost_ref, gate_ref, y_ref):
    x = x_ref[0]
    hb = (_rms(x) * gpre_ref[...] * (1.0 + sc_ref[0]) + sh_ref[0]).astype(BF16)
    d = x.shape[-1]
    acc = jnp.zeros(x.shape, F32)
    for j in range(wu_ref.shape[1] // d):
        u = jnp.maximum(_dot(hb, wu_ref[:, j * d:(j + 1) * d]), 0.0)
        acc = acc + _dot((u * u).astype(BF16), wd_ref[j * d:(j + 1) * d, :])
    y_ref[0] = _finish(acc, x, gpost_ref[...], gate_ref[0])


def _mlp(x, g_pre, shift, scale, w_up, w_down, g_post, gate):
    nb, s, d = x.shape
    tm = TOKEN_TILE
    tok = lambda b, i: (b, i, 0)
    vec = pl.BlockSpec((1, d), lambda b, i: (0, 0))
    mod = pl.BlockSpec((1, 1, d), lambda b, i: (b, 0, 0))
    return pl.pallas_call(
        _mlp_kernel,
        grid=(nb, s // tm),
        in_specs=[pl.BlockSpec((1, tm, d), tok), vec, mod, mod,
                  _resident(w_up.shape, lambda b, i: (0, 0)),
                  _resident(w_down.shape, lambda b, i: (0, 0)),
                  vec, mod],
        out_specs=pl.BlockSpec((1, tm, d), tok),
        out_shape=jax.ShapeDtypeStruct(x.shape, F32),
        compiler_params=_cparams(2),
        name="mlp",
    )(x, g_pre, shift, scale, w_up, w_down, g_post, gate)


def _proj1_kernel(x_ref, g_ref, sh_ref, sc_ref, w_ref, q_ref, k_ref, v_ref):
    hb = (_rms(x_ref[0]) * g_ref[...] * (1.0 + sc_ref[0]) + sh_ref[0]).astype(BF16)
    d = q_ref.shape[-1]
    q_ref[0] = (_dot(hb, w_ref[:, 0:d]) * QK_SCALE).astype(BF16)
    k_ref[0] = _dot(hb, w_ref[:, d:2 * d]).astype(BF16)
    v_ref[0] = _dot(hb, w_ref[:, 2 * d:3 * d]).astype(BF16)


def _proj1(x, g_pre, shift, scale, w_in):
    nb, s, d = x.shape
    tm = TOKEN_TILE
    n = w_in.shape[1] // 3
    tok = lambda b, i: (b, i, 0)
    mod = pl.BlockSpec((1, 1, d), lambda b, i: (b, 0, 0))
    out = jax.ShapeDtypeStruct((nb, s, n), BF16)
    return pl.pallas_call(
        _proj1_kernel,
        grid=(nb, s // tm),
        in_specs=[pl.BlockSpec((1, tm, d), tok),
                  pl.BlockSpec((1, d), lambda b, i: (0, 0)), mod, mod,
                  _resident(w_in.shape, lambda b, i: (0, 0))],
        out_specs=(pl.BlockSpec((1, tm, n), tok),) * 3,
        out_shape=(out, out, out),
        compiler_params=_cparams(2),
        name="proj1",
    )(x, g_pre, shift, scale, w_in)


def _attn_c_kernel(q_ref, k_ref, v_ref, bias_ref, o_ref):
    rows = k_ref.shape[1] // GRID_W
    pair_tokens = 2 * GRID_W
    win = C_WIN_ROWS * GRID_W
    blk = pl.program_id(2)
    for j in range(C_PAIRS):
        nominal = 2 * (blk * C_PAIRS + j) - NA_ROWS // 2
        ws = jnp.clip(nominal, 0, rows - C_WIN_ROWS)
        var = 2 - lax.shift_right_arithmetic(ws - nominal, 1)
        start = pl.multiple_of(ws * GRID_W, GRID_W)
        kw = k_ref[0, pl.ds(start, win), :]
        vw = v_ref[0, pl.ds(start, win), :]
        q = q_ref[0, j * pair_tokens:(j + 1) * pair_tokens, :]
        left, res = _pair_softmax_pv(q, kw, vw, lambda hd: bias_ref[hd, var])
        o_ref[0, j * pair_tokens:(j + 1) * pair_tokens, :] = jnp.where(left, res[0][0], res[1][0]).astype(BF16)


def _na_bias_table(rpb, rows):
    n_pairs = rows // 2
    tiles = []
    c = np.arange(GRID_W)
    cs = np.clip(c - NA_COLS // 2, 0, GRID_W - NA_COLS)
    kc = np.arange(GRID_W)
    valid_c = (kc[None, :] >= cs[:, None]) & (kc[None, :] < cs[:, None] + NA_COLS)
    dc = np.clip(kc[None, :] - c[:, None] + NA_COLS - 1, 0, 2 * NA_COLS - 2)
    for i in (0, 1, 2, n_pairs - 2, n_pairs - 1):
        ws = int(np.clip(2 * i - NA_ROWS // 2, 0, rows - C_WIN_ROWS))
        r = 2 * i + np.arange(2)
        kr = ws + np.arange(C_WIN_ROWS)
        rs = np.clip(r - NA_ROWS // 2, 0, rows - NA_ROWS)
        valid_r = (kr[None, :] >= rs[:, None]) & (kr[None, :] < rs[:, None] + NA_ROWS)
        dr = np.clip(kr[None, :] - r[:, None] + NA_ROWS - 1, 0, 2 * NA_ROWS - 2)
        vals = rpb[:, dr[:, None, :, None], dc[None, :, None, :]]
        valid = valid_r[:, None, :, None] & valid_c[None, :, None, :]
        tile = jnp.where(valid[None], vals.astype(F32), NEG_BIG)
        tiles.append(tile.reshape(rpb.shape[0], 2 * GRID_W, C_WIN_ROWS * GRID_W))
    return jnp.stack(tiles, axis=1)


def _attn_c(q, k, v, bias):
    nb, s, w = q.shape
    pairs = w // LANES
    step_tokens = C_PAIRS * 2 * GRID_W
    n_var, tq, win = bias.shape[1:]
    col = lambda b, hp, i: (b, 0, hp)
    return pl.pallas_call(
        _attn_c_kernel,
        grid=(nb, pairs, s // step_tokens),
        in_specs=[pl.BlockSpec((1, step_tokens, LANES), lambda b, hp, i: (b, i, hp)),
                  _resident((1, s, LANES), col),
                  _resident((1, s, LANES), col),
                  _resident((2, n_var, tq, win), lambda b, hp, i: (hp, 0, 0, 0))],
        out_specs=pl.BlockSpec((1, step_tokens, LANES), lambda b, hp, i: (b, i, hp)),
        out_shape=jax.ShapeDtypeStruct((nb, s, w), BF16),
        compiler_params=_cparams(3),
        name="attn_c",
    )(q, k, v, bias)


def _rope_tables(s):
    t = jnp.arange(s, dtype=jnp.int32)

    def cos_sin(pos, dim):
        inv_freq = ROPE_THETA ** (-jnp.arange(0, dim, 2, dtype=F32) / dim)
        ang = pos.astype(F32)[:, None] * inv_freq[None, :]
        return jnp.cos(ang), jnp.sin(ang)

    cr, sr = cos_sin(t // GRID_W, HEAD_DIM // 2)
    cc, sc = cos_sin(t % GRID_W, HEAD_DIM // 2)
    cos_a = jnp.concatenate([cr, cr, cc, cc] * 2, axis=-1)
    sin_a = jnp.concatenate([-sr, sr, -sc, sc] * 2, axis=-1)
    c1, s1 = cos_sin(t, HEAD_DIM)
    cos_b = jnp.concatenate([c1, c1] * 2, axis=-1)
    sin_b = jnp.concatenate([-s1, s1] * 2, axis=-1)
    return cos_a, sin_a, cos_b, sin_b


def _head_sum_matrix(n):
    idx = np.arange(n) // HEAD_DIM
    return jnp.asarray(idx[:, None] == idx[None, :], BF16)


def kernel(x, c, ada_w, ada_b, norm_g, ab_w_in, ab_w_out, a_q_gain, a_k_gain,
           c_w_in, c_w_out, c_rpb, mlp_w_up, mlp_w_down):
    nb, s, d = x.shape
    assert s % (C_PAIRS * 2 * GRID_W) == 0 and s // GRID_W >= 2 * C_WIN_ROWS
    mods = _adaln(c, ada_w, ada_b).reshape(ada_w.shape[0], 2, nb, 3, d)

    def mod(layer, which):
        m = mods[layer, which]
        return m[:, 0:1], m[:, 1:2], m[:, 2:3]

    ropes = _rope_tables(s)
    gain_qk = jnp.concatenate([jnp.tile(a_q_gain[0] * QK_SCALE, A_Q_HEADS),
                               jnp.tile(a_k_gain[0], A_KV_HEADS)])[None, :]
    gmat = _head_sum_matrix(A_Q_W + A_KV_W)

    shift, scale, gate = mod(0, 0)
    qa, ka, vt, qb, kb, vb = _proj0(x, norm_g[0, 0:1], shift, scale, ab_w_in[0].astype(BF16),
                                    ropes, gain_qk, gmat)
    oa = _attn_a(qa, ka, vt)
    groups = [_attn_b_group(qb, kb, vb, gi) for gi in range(len(B_GROUPS))]
    x = _out0(oa, groups, ab_w_out[0].astype(BF16), x, norm_g[0, 1:2], gate)
    shift, scale, gate = mod(0, 1)
    x = _mlp(x, norm_g[0, 2:3], shift, scale, mlp_w_up[0].astype(BF16), mlp_w_down[0].astype(BF16),
             norm_g[0, 3:4], gate)

    shift, scale, gate = mod(1, 0)
    q, k, v = _proj1(x, norm_g[1, 0:1], shift, scale, c_w_in[0].astype(BF16))
    oc = _attn_c(q, k, v, _na_bias_table(c_rpb[0], s // GRID_W))
    x = _out1(oc, c_w_out[0].astype(BF16), x, norm_g[1, 1:2], gate)
    shift, scale, gate = mod(1, 1)
    x = _mlp(x, norm_g[1, 2:3], shift, scale, mlp_w_up[1].astype(BF16), mlp_w_down[1].astype(BF16),
             norm_g[1, 3:4], gate)
    return x
```

```python
import functools
import math

import numpy as np
import jax
import jax.numpy as jnp
from jax import lax
from jax.experimental import pallas as pl
from jax.experimental.pallas import tpu as pltpu

F32 = jnp.float32
BF16 = jnp.bfloat16

HEAD_DIM = 64
GRID_W = 64
ROPE_THETA = 10000.0
RMS_EPS = 1e-6
A_Q_HEADS = 8
A_KV_HEADS = 2
B_GROUPS = ((128, 1), (512, 4), (2048, 16))
B_HEADS_PER_GROUP = 4
C_HEADS = 16
NA_ROWS = 8
NA_COLS = 16
A_Q_W = A_Q_HEADS * HEAD_DIM
A_KV_W = A_KV_HEADS * HEAD_DIM
B_GROUP_W = B_HEADS_PER_GROUP * HEAD_DIM
B_W = len(B_GROUPS) * B_GROUP_W
AB_A_W = A_Q_W + 2 * A_KV_W
QK_SCALE = 1.0 / math.sqrt(HEAD_DIM)
LOG2_E = math.log2(math.e)

LANES = 128
V7X_VMEM_LIMIT_BYTES = 56 * 1024 * 1024
NEG_BIG = -1e30

TOKEN_TILE = 512
A_TQ = 256
A_TK = 256
A_UNROLL = 8
A_LOOKAHEAD = 8
A_SCORE_RING = 16
A_V_ROWS = 80
B_TQ = 128
B_LOOKAHEAD = 2
C_LOOKAHEAD = 3
C_PAIRS = 8
C_WIN_ROWS = 10


def _cparams(n_axes):
    return pltpu.CompilerParams(dimension_semantics=("arbitrary",) * n_axes,
                                vmem_limit_bytes=V7X_VMEM_LIMIT_BYTES)


def _resident(block_shape, index_map):
    return pl.BlockSpec(block_shape, index_map, pipeline_mode=pl.Buffered(1))


def _rms(x):
    return x * lax.rsqrt(jnp.mean(x * x, axis=-1, keepdims=True) + RMS_EPS)


def _dot(a, b):
    return jnp.dot(a, b, preferred_element_type=F32)


def _dot_nt(a, b):
    return lax.dot_general(a, b, (((1,), (1,)), ((), ())), preferred_element_type=F32)


def _rope_slab(x, cos, sin_signed, half):
    lane = lax.broadcasted_iota(jnp.int32, x.shape, 1)
    first = (lane & half) == 0
    rot = jnp.where(first, pltpu.roll(x, LANES - half, 1), pltpu.roll(x, half, 1))
    return x * cos + rot * sin_signed


def _adaln_kernel(ct_ref, w_ref, b_ref, o_ref):
    ct = ct_ref[...]
    cond = ct * jax.nn.sigmoid(ct)
    w = w_ref[0]
    rows = [jnp.sum(w * cond[:, b:b + 1], axis=0, keepdims=True) for b in range(ct.shape[1])]
    o_ref[0] = jnp.concatenate(rows, axis=0) + b_ref[0]


def _adaln(c, ada_w, ada_b):
    nb, d = c.shape
    n_mod = ada_w.shape[0] * ada_w.shape[1]
    n_out = ada_w.shape[-1]
    tn = 768
    w = ada_w.reshape(n_mod, d, n_out)
    b = ada_b.reshape(n_mod, 1, n_out)
    return pl.pallas_call(
        _adaln_kernel,
        grid=(n_mod, n_out // tn),
        in_specs=[pl.BlockSpec((d, nb), lambda m, j: (0, 0)),
                  pl.BlockSpec((1, d, tn), lambda m, j: (m, 0, j)),
                  pl.BlockSpec((1, 1, tn), lambda m, j: (m, 0, j))],
        out_specs=pl.BlockSpec((1, nb, tn), lambda m, j: (m, 0, j)),
        out_shape=jax.ShapeDtypeStruct((n_mod, nb, n_out), F32),
        compiler_params=_cparams(2),
        name="adaln",
    )(c.T, w, b)


def _proj0_kernel(x_ref, g_ref, sh_ref, sc_ref, w_ref, ca_ref, sa_ref, cb_ref, sb_ref, gain_ref, gmat_ref,
                  qa_ref, ka_ref, vt_ref, qb_ref, kb_ref, vb_ref):
    tm = x_ref.shape[1]
    x = x_ref[0]
    h = _rms(x) * g_ref[...] * (1.0 + sc_ref[0]) + sh_ref[0]
    hb = h.astype(BF16)

    pa = _dot(hb, w_ref[:, 0:AB_A_W])
    n_qk = A_Q_W + A_KV_W
    qk = pa[:, 0:n_qk]
    sq = qk * qk
    hi = sq.astype(BF16)
    lo = (sq - hi.astype(F32)).astype(BF16)
    ms = (_dot(hi, gmat_ref[...]) + _dot(lo, gmat_ref[...])) * (1.0 / HEAD_DIM)
    qk = qk * lax.rsqrt(ms + RMS_EPS) * gain_ref[...]
    ca = ca_ref[...]
    sa = sa_ref[...]
    for j in range(n_qk // LANES):
        slab = _rope_slab(qk[:, j * LANES:(j + 1) * LANES], ca, sa, HEAD_DIM // 4).astype(BF16)
        if j < A_Q_W // LANES:
            qa_ref[0, :, j * LANES:(j + 1) * LANES] = slab
        else:
            ka_ref[0] = slab
    vt = pa[:, n_qk:AB_A_W].T
    ones = jnp.ones((A_V_ROWS - HEAD_DIM, A_TK), BF16)
    for kvh in range(A_KV_HEADS):
        for c in range(tm // A_TK):
            vt_ref[0, kvh, c, 0:HEAD_DIM, :] = vt[kvh * HEAD_DIM:(kvh + 1) * HEAD_DIM,
                                                  c * A_TK:(c + 1) * A_TK].astype(BF16)
            vt_ref[0, kvh, c, HEAD_DIM:A_V_ROWS, :] = ones

    pb = _dot(hb, w_ref[:, AB_A_W:AB_A_W + 3 * B_W])
    cb = cb_ref[...]
    sb = sb_ref[...]
    for j in range(B_W // LANES):
        q = _rope_slab(pb[:, j * LANES:(j + 1) * LANES], cb, sb, HEAD_DIM // 2) * QK_SCALE
        qb_ref[0, :, j * LANES:(j + 1) * LANES] = q.astype(BF16)
        k = _rope_slab(pb[:, B_W + j * LANES:B_W + (j + 1) * LANES], cb, sb, HEAD_DIM // 2)
        kb_ref[0, :, j * LANES:(j + 1) * LANES] = k.astype(BF16)
    vb_ref[0] = pb[:, 2 * B_W:3 * B_W].astype(BF16)


def _proj0(x, g_pre, shift, scale, w_in, ropes, gain_qk, gmat):
    nb, s, d = x.shape
    tm = TOKEN_TILE
    ca, sa, cb, sb = ropes
    tok = lambda b, i: (b, i, 0)
    mod = lambda b, i: (b, 0, 0)
    rope_spec = pl.BlockSpec((tm, LANES), lambda b, i: (i, 0))
    n_qk = A_Q_W + A_KV_W
    out_shape = (
        jax.ShapeDtypeStruct((nb, s, A_Q_W), BF16),
        jax.ShapeDtypeStruct((nb, s, A_KV_W), BF16),
        jax.ShapeDtypeStruct((nb, A_KV_HEADS, s // A_TK, A_V_ROWS, A_TK), BF16),
        jax.ShapeDtypeStruct((nb, s, B_W), BF16),
        jax.ShapeDtypeStruct((nb, s, B_W), BF16),
        jax.ShapeDtypeStruct((nb, s, B_W), BF16),
    )
    out_specs = (
        pl.BlockSpec((1, tm, A_Q_W), tok),
        pl.BlockSpec((1, tm, A_KV_W), tok),
        pl.BlockSpec((1, A_KV_HEADS, tm // A_TK, A_V_ROWS, A_TK), lambda b, i: (b, 0, i, 0, 0)),
        pl.BlockSpec((1, tm, B_W), tok),
        pl.BlockSpec((1, tm, B_W), tok),
        pl.BlockSpec((1, tm, B_W), tok),
    )
    return pl.pallas_call(
        _proj0_kernel,
        grid=(nb, s // tm),
        in_specs=[pl.BlockSpec((1, tm, d), tok),
                  pl.BlockSpec((1, d), lambda b, i: (0, 0)),
                  pl.BlockSpec((1, 1, d), mod),
                  pl.BlockSpec((1, 1, d), mod),
                  _resident(w_in.shape, lambda b, i: (0, 0)),
                  rope_spec, rope_spec, rope_spec, rope_spec,
                  pl.BlockSpec((1, n_qk), lambda b, i: (0, 0)),
                  _resident((n_qk, n_qk), lambda b, i: (0, 0))],
        out_specs=out_specs,
        out_shape=out_shape,
        compiler_params=_cparams(2),
        name="proj0",
    )(x, g_pre, shift, scale, w_in, ca, sa, cb, sb, gain_qk, gmat)


def _attn_a_kernel(q_ref, k_ref, vt_ref, o_ref, qbd_ref, s_ref):
    tq = q_ref.shape[1]
    n_chunks = k_ref.shape[1] // A_TK
    qt = q_ref[0].astype(F32).T
    zeros = jnp.zeros((HEAD_DIM, tq), F32)
    group = A_Q_HEADS // A_KV_HEADS
    for h in range(A_Q_HEADS):
        blk = qt[h * HEAD_DIM:(h + 1) * HEAD_DIM]
        full = jnp.concatenate([blk, zeros] if h < group else [zeros, blk], axis=0)
        qbd_ref[h] = full.astype(BF16)

    strips = [(u, h) for u in range(A_UNROLL) for h in range(A_Q_HEADS)]
    n_strips = len(strips)
    ring = s_ref.shape[0]
    assert n_strips % ring == 0 and A_LOOKAHEAD < ring and n_chunks % A_UNROLL == 0

    def issue_score(slot, chunk, h):
        k = k_ref[0, pl.ds(pl.multiple_of(chunk * A_TK, A_TK), A_TK), :]
        s_ref[slot] = _dot(k, qbd_ref[h])

    def body(c, carry):
        ms, accs = list(carry[0]), list(carry[1])
        next_base = jnp.minimum(c + 1, n_chunks // A_UNROLL - 1) * A_UNROLL
        for idx, (u, h) in enumerate(strips):
            nxt = idx + A_LOOKAHEAD
            if nxt < n_strips:
                issue_score(nxt % ring, c * A_UNROLL + strips[nxt][0], strips[nxt][1])
            else:
                u2, h2 = strips[nxt - n_strips]
                issue_score(nxt % ring, next_base + u2, h2)
            slot = idx % ring
            m_new = jnp.maximum(ms[h], jnp.max(s_ref[slot], axis=0, keepdims=True))
            alpha = jnp.exp2(ms[h] - m_new)
            p = jnp.exp2(s_ref[slot] - m_new).astype(BF16)
            pv = _dot(vt_ref[0, h // group, c * A_UNROLL + u], p)
            ms[h] = m_new
            accs[h] = accs[h] * alpha + pv
        return tuple(ms), tuple(accs)

    for idx in range(A_LOOKAHEAD):
        issue_score(idx, strips[idx][0], strips[idx][1])
    init = (tuple(jnp.full((1, tq), -jnp.inf, F32) for _ in range(A_Q_HEADS)),
            tuple(jnp.zeros((A_V_ROWS, tq), F32) for _ in range(A_Q_HEADS)))
    _, accs = lax.fori_loop(0, n_chunks // A_UNROLL, body, init)
    outs = []
    for h in range(A_Q_HEADS):
        a = accs[h]
        outs.append(a[0:HEAD_DIM] / a[HEAD_DIM:HEAD_DIM + 1])
    o_ref[0] = jnp.concatenate(outs, axis=0).T.astype(BF16)


def _attn_a(qa, ka, vt):
    nb, s, _ = qa.shape
    tq = A_TQ
    return pl.pallas_call(
        _attn_a_kernel,
        grid=(nb, s // tq),
        in_specs=[pl.BlockSpec((1, tq, A_Q_W), lambda b, i: (b, i, 0)),
                  _resident((1, s, A_KV_W), lambda b, i: (b, 0, 0)),
                  _resident((1,) + vt.shape[1:], lambda b, i: (b, 0, 0, 0, 0))],
        out_specs=pl.BlockSpec((1, tq, A_Q_W), lambda b, i: (b, i, 0)),
        out_shape=jax.ShapeDtypeStruct((nb, s, A_Q_W), BF16),
        scratch_shapes=[pltpu.VMEM((A_Q_HEADS, 2 * HEAD_DIM, tq), BF16),
                        pltpu.VMEM((A_SCORE_RING, A_TK, tq), F32)],
        compiler_params=_cparams(2),
        name="attn_a",
    )(qa, ka, vt)


def _left_lanes(shape):
    return lax.broadcasted_iota(jnp.int32, shape, 1) < HEAD_DIM


def _head_scores(q, kw, hd):
    left = _left_lanes(q.shape)
    zero = jnp.zeros_like(q)
    qh = jnp.where(left, q, zero) if hd == 0 else jnp.where(left, zero, q)
    return _dot_nt(qh, kw)


def _softmax_pv(s, vw):
    m = jnp.max(s, axis=-1, keepdims=True)
    p = jnp.exp(s - m)
    l = jnp.sum(p, axis=-1, keepdims=True)
    o = _dot(p.astype(BF16), vw) * (1.0 / l)
    return o, m + jnp.log(l)


def _run_tasks(tasks, lookahead, score, finish):
    pending = [score(t) for t in tasks[:lookahead]]
    for idx, t in enumerate(tasks):
        if idx + lookahead < len(tasks):
            pending.append(score(tasks[idx + lookahead]))
        finish(t, pending.pop(0))


def _attn_b_kernel(q_ref, k_ref, v_ref, bias_ref, o_ref, lse_ref, *, half_w, win):
    tq = q_ref.shape[1]
    seq = k_ref.shape[1]
    t0 = pl.program_id(1) * tq
    start = pl.multiple_of(jnp.clip(t0 - half_w, 0, seq - win), HEAD_DIM)
    pairs = q_ref.shape[2] // LANES
    left = _left_lanes((tq, LANES))
    lanes = lambda hp: slice(hp * LANES, (hp + 1) * LANES)

    def score(task):
        hp, hd = task
        return _head_scores(q_ref[0, :, lanes(hp)], k_ref[0, pl.ds(start, win), lanes(hp)], hd) + bias_ref[0]

    first = {}

    def finish(task, s):
        hp, hd = task
        o, lse = _softmax_pv(s, v_ref[0, pl.ds(start, win), lanes(hp)])
        if hd == 0:
            first[hp] = (o, lse)
        else:
            o0, lse0 = first.pop(hp)
            o_ref[0, :, lanes(hp)] = jnp.where(left, o0, o)
            lse_ref[0, :, lanes(hp)] = jnp.where(left, lse0, lse)

    _run_tasks([(hp, hd) for hp in range(pairs) for hd in range(2)], B_LOOKAHEAD, score, finish)


def _dilated_plan(s, window, dilation, tq):
    half_w = window // 2
    win = tq + 2 * half_w
    assert win <= s, "sequence shorter than one dilated key window"
    nblk = s // tq
    n_edge = -(-half_w // tq)
    shifts = [half_w - i * tq for i in range(n_edge)] + [0] + \
             [(s - win) - ((nblk - n_edge + i) * tq - half_w) for i in range(n_edge)]
    sh = jnp.asarray(shifts, jnp.int32)[:, None, None]
    qi = jnp.arange(tq, dtype=jnp.int32)[None, :, None]
    kj = jnp.arange(win, dtype=jnp.int32)[None, None, :]
    delta = kj - qi + sh - half_w
    valid = (jnp.abs(delta) <= half_w) & ((delta & (dilation - 1)) == 0)
    bias = jnp.where(valid, 0.0, NEG_BIG).astype(F32)

    def variant(i):
        return jnp.where(i < n_edge, i, jnp.where(i >= nblk - n_edge, i - (nblk - 2 * n_edge) + 1, n_edge))

    return half_w, win, bias, variant


def _attn_b_group(qb, kb, vb, gi):
    nb, s, _ = qb.shape
    window, dilation = B_GROUPS[gi]
    tq = B_TQ
    half_w, win, bias, variant = _dilated_plan(s, window, dilation, tq)
    kern = functools.partial(_attn_b_kernel, half_w=half_w, win=win)
    tile = pl.BlockSpec((1, tq, B_GROUP_W), lambda b, i: (b, i, gi))
    whole = _resident((1, s, B_GROUP_W), lambda b, i: (b, 0, gi))
    out = pl.BlockSpec((1, tq, B_GROUP_W), lambda b, i: (b, i, 0))
    return pl.pallas_call(
        kern,
        grid=(nb, s // tq),
        in_specs=[tile, whole, whole,
                  pl.BlockSpec((1, tq, win), lambda b, i: (variant(i), 0, 0))],
        out_specs=(out, out),
        out_shape=(jax.ShapeDtypeStruct((nb, s, B_GROUP_W), F32),
                   jax.ShapeDtypeStruct((nb, s, B_GROUP_W), F32)),
        compiler_params=_cparams(2),
        name=f"attn_b{gi}",
    )(qb, kb, vb, bias)


def _finish(y, x, g_post, gate):
    return x + gate * (_rms(y) * g_post)


def _out0_kernel(oa_ref, o0_ref, l0_ref, o1_ref, l1_ref, o2_ref, l2_ref, wa_ref, wb_ref,
                 x_ref, g_ref, gate_ref, y_ref):
    l0, l1, l2 = l0_ref[0], l1_ref[0], l2_ref[0]
    m = jnp.maximum(jnp.maximum(l0, l1), l2)
    e0, e1, e2 = jnp.exp(l0 - m), jnp.exp(l1 - m), jnp.exp(l2 - m)
    ob = (e0 * o0_ref[0] + e1 * o1_ref[0] + e2 * o2_ref[0]) / (e0 + e1 + e2)
    y = _dot(oa_ref[0], wa_ref[...]) + _dot(ob.astype(BF16), wb_ref[...])
    y_ref[0] = _finish(y, x_ref[0], g_ref[...], gate_ref[0])


def _out0(oa, groups, w_out, x, g_post, gate):
    nb, s, d = x.shape
    tm = TOKEN_TILE
    tok = lambda b, i: (b, i, 0)
    grp_spec = pl.BlockSpec((1, tm, B_GROUP_W), tok)
    flat = [t for pair in groups for t in pair]
    wa, wb = w_out[:A_Q_W], w_out[A_Q_W:]
    return pl.pallas_call(
        _out0_kernel,
        grid=(nb, s // tm),
        in_specs=[pl.BlockSpec((1, tm, A_Q_W), tok)] + [grp_spec] * 6 +
                 [_resident(wa.shape, lambda b, i: (0, 0)),
                  _resident(wb.shape, lambda b, i: (0, 0)),
                  pl.BlockSpec((1, tm, d), tok),
                  pl.BlockSpec((1, d), lambda b, i: (0, 0)),
                  pl.BlockSpec((1, 1, d), lambda b, i: (b, 0, 0))],
        out_specs=pl.BlockSpec((1, tm, d), tok),
        out_shape=jax.ShapeDtypeStruct(x.shape, F32),
        compiler_params=_cparams(2),
        name="out0",
    )(oa, *flat, wa, wb, x, g_post, gate)


def _out1_kernel(o_ref, w_ref, x_ref, g_ref, gate_ref, y_ref):
    y_ref[0] = _finish(_dot(o_ref[0], w_ref[...]), x_ref[0], g_ref[...], gate_ref[0])


def _out1(o, w_out, x, g_post, gate):
    nb, s, d = x.shape
    tm = TOKEN_TILE
    tok = lambda b, i: (b, i, 0)
    return pl.pallas_call(
        _out1_kernel,
        grid=(nb, s // tm),
        in_specs=[pl.BlockSpec((1, tm, o.shape[-1]), tok),
                  _resident(w_out.shape, lambda b, i: (0, 0)),
                  pl.BlockSpec((1, tm, d), tok),
                  pl.BlockSpec((1, d), lambda b, i: (0, 0)),
                  pl.BlockSpec((1, 1, d), lambda b, i: (b, 0, 0))],
        out_specs=pl.BlockSpec((1, tm, d), tok),
        out_shape=jax.ShapeDtypeStruct(x.shape, F32),
        compiler_params=_cparams(2),
        name="out1",
    )(o, w_out, x, g_post, gate)


def _mlp_kernel(x_ref, gpre_ref, sh_ref, sc_ref, wu_ref, wd_ref, gpost_ref, gate_ref, y_ref):
    x = x_ref[0]
    hb = (_rms(x) * gpre_ref[...] * (1.0 + sc_ref[0]) + sh_ref[0]).astype(BF16)
    d = x.shape[-1]
    acc = jnp.zeros(x.shape, F32)
    for j in range(wu_ref.shape[1] // d):
        u = jnp.maximum(_dot(hb, wu_ref[:, j * d:(j + 1) * d]), 0.0)
        acc = acc + _dot((u * u).astype(BF16), wd_ref[j * d:(j + 1) * d, :])
    y_ref[0] = _finish(acc, x, gpost_ref[...], gate_ref[0])


def _mlp(x, g_pre, shift, scale, w_up, w_down, g_post, gate):
    nb, s, d = x.shape
    tm = TOKEN_TILE
    tok = lambda b, i: (b, i, 0)
    vec = pl.BlockSpec((1, d), lambda b, i: (0, 0))
    mod = pl.BlockSpec((1, 1, d), lambda b, i: (b, 0, 0))
    return pl.pallas_call(
        _mlp_kernel,
        grid=(nb, s // tm),
        in_specs=[pl.BlockSpec((1, tm, d), tok), vec, mod, mod,
                  _resident(w_up.shape, lambda b, i: (0, 0)),
                  _resident(w_down.shape, lambda b, i: (0, 0)),
                  vec, mod],
        out_specs=pl.BlockSpec((1, tm, d), tok),
        out_shape=jax.ShapeDtypeStruct(x.shape, F32),
        compiler_params=_cparams(2),
        name="mlp",
    )(x, g_pre, shift, scale, w_up, w_down, g_post, gate)


def _proj1_kernel(x_ref, g_ref, sh_ref, sc_ref, w_ref, q_ref, k_ref, v_ref):
    hb = (_rms(x_ref[0]) * g_ref[...] * (1.0 + sc_ref[0]) + sh_ref[0]).astype(BF16)
    d = q_ref.shape[-1]
    q_ref[0] = (_dot(hb, w_ref[:, 0:d]) * QK_SCALE).astype(BF16)
    k_ref[0] = _dot(hb, w_ref[:, d:2 * d]).astype(BF16)
    v_ref[0] = _dot(hb, w_ref[:, 2 * d:3 * d]).astype(BF16)


def _proj1(x, g_pre, shift, scale, w_in):
    nb, s, d = x.shape
    tm = TOKEN_TILE
    n = w_in.shape[1] // 3
    tok = lambda b, i: (b, i, 0)
    mod = pl.BlockSpec((1, 1, d), lambda b, i: (b, 0, 0))
    out = jax.ShapeDtypeStruct((nb, s, n), BF16)
    return pl.pallas_call(
        _proj1_kernel,
        grid=(nb, s // tm),
        in_specs=[pl.BlockSpec((1, tm, d), tok),
                  pl.BlockSpec((1, d), lambda b, i: (0, 0)), mod, mod,
                  _resident(w_in.shape, lambda b, i: (0, 0))],
        out_specs=(pl.BlockSpec((1, tm, n), tok),) * 3,
        out_shape=(out, out, out),
        compiler_params=_cparams(2),
        name="proj1",
    )(x, g_pre, shift, scale, w_in)


def _attn_c_kernel(q_ref, k_ref, v_ref, bias_ref, o_ref):
    rows = k_ref.shape[1] // GRID_W
    pair_tokens = 2 * GRID_W
    win = C_WIN_ROWS * GRID_W
    blk = pl.program_id(2)
    left = _left_lanes((pair_tokens, LANES))
    starts, variants = [], []
    for j in range(C_PAIRS):
        nominal = 2 * (blk * C_PAIRS + j) - NA_ROWS // 2
        ws = jnp.clip(nominal, 0, rows - C_WIN_ROWS)
        variants.append(2 - lax.shift_right_arithmetic(ws - nominal, 1))
        starts.append(pl.multiple_of(ws * GRID_W, GRID_W))
    toks = lambda j: slice(j * pair_tokens, (j + 1) * pair_tokens)

    def score(task):
        j, hd = task
        return _head_scores(q_ref[0, toks(j), :], k_ref[0, pl.ds(starts[j], win), :], hd) + bias_ref[hd, variants[j]]

    first = {}

    def finish(task, s):
        j, hd = task
        o, _ = _softmax_pv(s, v_ref[0, pl.ds(starts[j], win), :])
        if hd == 0:
            first[j] = o
        else:
            o_ref[0, toks(j), :] = jnp.where(left, first.pop(j), o).astype(BF16)

    _run_tasks([(j, hd) for j in range(C_PAIRS) for hd in range(2)], C_LOOKAHEAD, score, finish)


def _na_bias_table(rpb, rows):
    n_heads, n_dr, n_dc = rpb.shape
    n_pairs = rows // 2
    c = np.arange(GRID_W)
    cs = np.clip(c - NA_COLS // 2, 0, GRID_W - NA_COLS)
    kc = np.arange(GRID_W)
    valid_c = (kc[None, :] >= cs[:, None]) & (kc[None, :] < cs[:, None] + NA_COLS)
    span = 2 * GRID_W - 1
    side = (span - n_dc) // 2
    g = jnp.pad(rpb.astype(F32), ((0, 0), (0, 0), (side, side)), constant_values=NEG_BIG)
    flat = jnp.tile(g, (1, 1, GRID_W))
    toep = flat[:, :, GRID_W - 1:GRID_W - 1 + GRID_W * (span - 1)]
    toep = toep.reshape(n_heads, n_dr, GRID_W, span - 1)[..., :GRID_W]
    toep = jnp.where(valid_c[None, None], toep, NEG_BIG)
    masked = jnp.full((n_heads, GRID_W, GRID_W), NEG_BIG, F32)
    tiles = []
    for i in (0, 1, 2, n_pairs - 2, n_pairs - 1):
        ws = int(np.clip(2 * i - NA_ROWS // 2, 0, rows - C_WIN_ROWS))
        halves = []
        for qr in range(2):
            r = 2 * i + qr
            rs = int(np.clip(r - NA_ROWS // 2, 0, rows - NA_ROWS))
            blocks = []
            for kr in range(ws, ws + C_WIN_ROWS):
                blocks.append(toep[:, kr - r + NA_ROWS - 1] if rs <= kr < rs + NA_ROWS else masked)
            halves.append(jnp.concatenate(blocks, axis=-1))
        tiles.append(jnp.concatenate(halves, axis=1))
    return jnp.stack(tiles, axis=1)


def _attn_c(q, k, v, bias):
    nb, s, w = q.shape
    pairs = w // LANES
    step_tokens = C_PAIRS * 2 * GRID_W
    n_var, tq, win = bias.shape[1:]
    col = lambda b, hp, i: (b, 0, hp)
    return pl.pallas_call(
        _attn_c_kernel,
        grid=(nb, pairs, s // step_tokens),
        in_specs=[pl.BlockSpec((1, step_tokens, LANES), lambda b, hp, i: (b, i, hp)),
                  _resident((1, s, LANES), col),
                  _resident((1, s, LANES), col),
                  _resident((2, n_var, tq, win), lambda b, hp, i: (hp, 0, 0, 0))],
        out_specs=pl.BlockSpec((1, step_tokens, LANES), lambda b, hp, i: (b, i, hp)),
        out_shape=jax.ShapeDtypeStruct((nb, s, w), BF16),
        compiler_params=_cparams(3),
        name="attn_c",
    )(q, k, v, bias)


def _rope_tables(s):
    t = jnp.arange(s, dtype=jnp.int32)

    def cos_sin(pos, dim):
        inv_freq = ROPE_THETA ** (-jnp.arange(0, dim, 2, dtype=F32) / dim)
        ang = pos.astype(F32)[:, None] * inv_freq[None, :]
        return jnp.cos(ang), jnp.sin(ang)

    cr, sr = cos_sin(t // GRID_W, HEAD_DIM // 2)
    cc, sc = cos_sin(t % GRID_W, HEAD_DIM // 2)
    cos_a = jnp.concatenate([cr, cr, cc, cc] * 2, axis=-1)
    sin_a = jnp.concatenate([-sr, sr, -sc, sc] * 2, axis=-1)
    c1, s1 = cos_sin(t, HEAD_DIM)
    cos_b = jnp.concatenate([c1, c1] * 2, axis=-1)
    sin_b = jnp.concatenate([-s1, s1] * 2, axis=-1)
    return cos_a, sin_a, cos_b, sin_b


def _head_sum_matrix(n):
    idx = np.arange(n) // HEAD_DIM
    return jnp.asarray(idx[:, None] == idx[None, :], BF16)


def kernel(x, c, ada_w, ada_b, norm_g, ab_w_in, ab_w_out, a_q_gain, a_k_gain,
           c_w_in, c_w_out, c_rpb, mlp_w_up, mlp_w_down):
    nb, s, d = x.shape
    assert s % (C_PAIRS * 2 * GRID_W) == 0 and s // GRID_W >= 2 * C_WIN_ROWS
    mods = _adaln(c, ada_w, ada_b).reshape(ada_w.shape[0], 2, nb, 3, d)

    def mod(layer, which):
        m = mods[layer, which]
        return m[:, 0:1], m[:, 1:2], m[:, 2:3]

    ropes = _rope_tables(s)
    gain_qk = jnp.concatenate([jnp.tile(a_q_gain[0] * (QK_SCALE * LOG2_E), A_Q_HEADS),
                               jnp.tile(a_k_gain[0], A_KV_HEADS)])[None, :]
    gmat = _head_sum_matrix(A_Q_W + A_KV_W)

    shift, scale, gate = mod(0, 0)
    qa, ka, vt, qb, kb, vb = _proj0(x, norm_g[0, 0:1], shift, scale, ab_w_in[0].astype(BF16),
                                    ropes, gain_qk, gmat)
    oa = _attn_a(qa, ka, vt)
    groups = [_attn_b_group(qb, kb, vb, gi) for gi in range(len(B_GROUPS))]
    x = _out0(oa, groups, ab_w_out[0].astype(BF16), x, norm_g[0, 1:2], gate)
    shift, scale, gate = mod(0, 1)
    x = _mlp(x, norm_g[0, 2:3], shift, scale, mlp_w_up[0].astype(BF16), mlp_w_down[0].astype(BF16),
             norm_g[0, 3:4], gate)

    shift, scale, gate = mod(1, 0)
    q, k, v = _proj1(x, norm_g[1, 0:1], shift, scale, c_w_in[0].astype(BF16))
    oc = _attn_c(q, k, v, _na_bias_table(c_rpb[0], s // GRID_W))
    x = _out1(oc, c_w_out[0].astype(BF16), x, norm_g[1, 1:2], gate)
    shift, scale, gate = mod(1, 1)
    x = _mlp(x, norm_g[1, 2:3], shift, scale, mlp_w_up[1].astype(BF16), mlp_w_down[1].astype(BF16),
             norm_g[1, 3:4], gate)
    return x
```

```python
import functools
import math

import numpy as np
import jax
import jax.numpy as jnp
from jax import lax
from jax.experimental import pallas as pl
from jax.experimental.pallas import tpu as pltpu

F32 = jnp.float32
BF16 = jnp.bfloat16

HEAD_DIM = 64
GRID_W = 64
ROPE_THETA = 10000.0
RMS_EPS = 1e-6
A_Q_HEADS = 8
A_KV_HEADS = 2
B_GROUPS = ((128, 1), (512, 4), (2048, 16))
B_HEADS_PER_GROUP = 4
C_HEADS = 16
NA_ROWS = 8
NA_COLS = 16
A_Q_W = A_Q_HEADS * HEAD_DIM
A_KV_W = A_KV_HEADS * HEAD_DIM
B_GROUP_W = B_HEADS_PER_GROUP * HEAD_DIM
B_W = len(B_GROUPS) * B_GROUP_W
AB_A_W = A_Q_W + 2 * A_KV_W
QK_SCALE = 1.0 / math.sqrt(HEAD_DIM)
LOG2_E = math.log2(math.e)

LANES = 128
V7X_VMEM_LIMIT_BYTES = 56 * 1024 * 1024
NEG_BIG = -1e30

TOKEN_TILE = 512
A_TQ = 256
A_TK = 256
A_UNROLL = 8
A_LOOKAHEAD = 8
A_SCORE_RING = 16
A_V_ROWS = 80
B_TILE = 2048
B_TQ = 128
B_HALF = 64
B_SUBS_PER_ITER = 4
B_LOOKAHEAD = 4
C_LOOKAHEAD = 3
C_PAIRS = 8
C_WIN_ROWS = 10


def _cparams(n_axes):
    return pltpu.CompilerParams(dimension_semantics=("arbitrary",) * n_axes,
                                vmem_limit_bytes=V7X_VMEM_LIMIT_BYTES)


def _resident(block_shape, index_map):
    return pl.BlockSpec(block_shape, index_map, pipeline_mode=pl.Buffered(1))


def _rms(x):
    return x * lax.rsqrt(jnp.mean(x * x, axis=-1, keepdims=True) + RMS_EPS)


def _dot(a, b):
    return jnp.dot(a, b, preferred_element_type=F32)


def _dot_nt(a, b):
    return lax.dot_general(a, b, (((1,), (1,)), ((), ())), preferred_element_type=F32)


def _rope_slab(x, cos, sin_signed, half):
    lane = lax.broadcasted_iota(jnp.int32, x.shape, 1)
    first = (lane & half) == 0
    rot = jnp.where(first, pltpu.roll(x, LANES - half, 1), pltpu.roll(x, half, 1))
    return x * cos + rot * sin_signed


def _adaln_kernel(ct_ref, w_ref, b_ref, o_ref):
    ct = ct_ref[...]
    cond = ct * jax.nn.sigmoid(ct)
    w = w_ref[0]
    rows = [jnp.sum(w * cond[:, b:b + 1], axis=0, keepdims=True) for b in range(ct.shape[1])]
    o_ref[0] = jnp.concatenate(rows, axis=0) + b_ref[0]


def _adaln(c, ada_w, ada_b):
    nb, d = c.shape
    n_mod = ada_w.shape[0] * ada_w.shape[1]
    n_out = ada_w.shape[-1]
    tn = 768
    w = ada_w.reshape(n_mod, d, n_out)
    b = ada_b.reshape(n_mod, 1, n_out)
    return pl.pallas_call(
        _adaln_kernel,
        grid=(n_mod, n_out // tn),
        in_specs=[pl.BlockSpec((d, nb), lambda m, j: (0, 0)),
                  pl.BlockSpec((1, d, tn), lambda m, j: (m, 0, j)),
                  pl.BlockSpec((1, 1, tn), lambda m, j: (m, 0, j))],
        out_specs=pl.BlockSpec((1, nb, tn), lambda m, j: (m, 0, j)),
        out_shape=jax.ShapeDtypeStruct((n_mod, nb, n_out), F32),
        compiler_params=_cparams(2),
        name="adaln",
    )(c.T, w, b)


def _proj0_kernel(x_ref, g_ref, sh_ref, sc_ref, w_ref, rc_ref, rs_ref, cc_ref, cs_ref, c0_ref, s0_ref, cj_ref, sj_ref,
                  gain_ref, gmat_ref, qa_ref, ka_ref, vt_ref, *rest):
    b_refs, stage_ref = rest[:-1], rest[-1]
    tm = x_ref.shape[1]
    col_c, col_s = cc_ref[...], cs_ref[...]
    ca = jnp.concatenate([rc_ref[r:r + 1, :] + col_c for r in range(tm // GRID_W)], axis=0)
    sa = jnp.concatenate([rs_ref[r:r + 1, :] + col_s for r in range(tm // GRID_W)], axis=0)
    c0, s0, cj, sj = c0_ref[0], s0_ref[0], cj_ref[...], sj_ref[...]
    lane = lax.broadcasted_iota(jnp.int32, (tm, LANES), 1)
    cb = c0 * cj - s0 * sj
    sb = jnp.where((lane & (HEAD_DIM // 2)) == 0, -1.0, 1.0) * (s0 * cj + c0 * sj)
    x = x_ref[0]
    h = _rms(x) * g_ref[...] * (1.0 + sc_ref[0]) + sh_ref[0]
    hb = h.astype(BF16)

    pa = _dot(hb, w_ref[:, 0:AB_A_W])
    n_qk = A_Q_W + A_KV_W
    qk = pa[:, 0:n_qk]
    sq = qk * qk
    hi = sq.astype(BF16)
    lo = (sq - hi.astype(F32)).astype(BF16)
    ms = (_dot(hi, gmat_ref[...]) + _dot(lo, gmat_ref[...])) * (1.0 / HEAD_DIM)
    qk = qk * lax.rsqrt(ms + RMS_EPS) * gain_ref[...]
    for j in range(n_qk // LANES):
        slab = _rope_slab(qk[:, j * LANES:(j + 1) * LANES], ca, sa, HEAD_DIM // 4).astype(BF16)
        if j < A_Q_W // LANES:
            qa_ref[0, :, j * LANES:(j + 1) * LANES] = slab
        else:
            ka_ref[0] = slab
    vt = pa[:, n_qk:AB_A_W].T
    ones = jnp.ones((A_V_ROWS - HEAD_DIM, A_TK), BF16)
    for kvh in range(A_KV_HEADS):
        for c in range(tm // A_TK):
            vt_ref[0, kvh, c, 0:HEAD_DIM, :] = vt[kvh * HEAD_DIM:(kvh + 1) * HEAD_DIM,
                                                  c * A_TK:(c + 1) * A_TK].astype(BF16)
            vt_ref[0, kvh, c, HEAD_DIM:A_V_ROWS, :] = ones

    pb = _dot(hb, w_ref[:, AB_A_W:AB_A_W + 3 * B_W])
    slot = 0
    for kind in range(3):
        for g, (_, dil) in enumerate(B_GROUPS):
            out_ref = b_refs[3 * g + kind]
            for hp in range(B_GROUP_W // LANES):
                lanes = slice(hp * LANES, (hp + 1) * LANES)
                col = kind * B_W + g * B_GROUP_W + hp * LANES
                slab = pb[:, col:col + LANES]
                if kind == 0:
                    slab = _rope_slab(slab, cb, sb, HEAD_DIM // 2) * (QK_SCALE * LOG2_E)
                elif kind == 1:
                    slab = _rope_slab(slab, cb, sb, HEAD_DIM // 2)
                if dil == 1:
                    out_ref[0, 0, :, lanes] = slab.astype(BF16)
                    continue
                stage_ref[slot] = slab
                for rho in range(dil):
                    out_ref[0, rho, :, lanes] = stage_ref[slot, pl.ds(rho, tm // dil, stride=dil), :].astype(BF16)
                slot += 1


def _proj0(x, g_pre, shift, scale, w_in, ropes, gain_qk, gmat):
    nb, s, d = x.shape
    tm = TOKEN_TILE
    tok = lambda b, i: (b, i, 0)
    mod = lambda b, i: (b, 0, 0)
    row_spec = pl.BlockSpec((tm // GRID_W, LANES), lambda b, i: (i, 0))
    col_spec = pl.BlockSpec((GRID_W, LANES), lambda b, i: (0, 0))
    base_spec = pl.BlockSpec((1, 1, LANES), lambda b, i: (i, 0, 0))
    in_tile_spec = pl.BlockSpec((tm, LANES), lambda b, i: (0, 0))
    n_qk = A_Q_W + A_KV_W
    out_shape = [
        jax.ShapeDtypeStruct((nb, s, A_Q_W), BF16),
        jax.ShapeDtypeStruct((nb, s, A_KV_W), BF16),
        jax.ShapeDtypeStruct((nb, A_KV_HEADS, s // A_TK, A_V_ROWS, A_TK), BF16),
    ]
    out_specs = [
        pl.BlockSpec((1, tm, A_Q_W), tok),
        pl.BlockSpec((1, tm, A_KV_W), tok),
        pl.BlockSpec((1, A_KV_HEADS, tm // A_TK, A_V_ROWS, A_TK), lambda b, i: (b, 0, i, 0, 0)),
    ]
    n_staged = 0
    for _, dil in B_GROUPS:
        for _ in range(3):
            out_shape.append(jax.ShapeDtypeStruct((nb, dil, s // dil, B_GROUP_W), BF16))
            out_specs.append(pl.BlockSpec((1, dil, tm // dil, B_GROUP_W), lambda b, i: (b, 0, i, 0)))
        n_staged += 3 * (B_GROUP_W // LANES) * (dil > 1)
    outs = pl.pallas_call(
        _proj0_kernel,
        grid=(nb, s // tm),
        in_specs=[pl.BlockSpec((1, tm, d), tok),
                  pl.BlockSpec((1, d), lambda b, i: (0, 0)),
                  pl.BlockSpec((1, 1, d), mod),
                  pl.BlockSpec((1, 1, d), mod),
                  _resident(w_in.shape, lambda b, i: (0, 0)),
                  row_spec, row_spec, col_spec, col_spec, base_spec, base_spec, in_tile_spec, in_tile_spec,
                  pl.BlockSpec((1, n_qk), lambda b, i: (0, 0)),
                  _resident((n_qk, n_qk), lambda b, i: (0, 0))],
        out_specs=out_specs,
        out_shape=out_shape,
        scratch_shapes=[pltpu.VMEM((n_staged, tm, LANES), F32)],
        compiler_params=_cparams(2),
        name="proj0",
    )(x, g_pre, shift, scale, w_in, *ropes, gain_qk, gmat)
    qa, ka, vt = outs[:3]
    groups = [tuple(outs[3 + 3 * g:6 + 3 * g]) for g in range(len(B_GROUPS))]
    return qa, ka, vt, groups


def _attn_a_kernel(q_ref, k_ref, vt_ref, o_ref, qbd_ref, s_ref):
    tq = q_ref.shape[1]
    n_chunks = k_ref.shape[1] // A_TK
    qt = q_ref[0].astype(F32).T
    zeros = jnp.zeros((HEAD_DIM, tq), F32)
    group = A_Q_HEADS // A_KV_HEADS
    for h in range(A_Q_HEADS):
        blk = qt[h * HEAD_DIM:(h + 1) * HEAD_DIM]
        full = jnp.concatenate([blk, zeros] if h < group else [zeros, blk], axis=0)
        qbd_ref[h] = full.astype(BF16)

    strips = [(u, h) for u in range(A_UNROLL) for h in range(A_Q_HEADS)]
    n_strips = len(strips)
    ring = s_ref.shape[0]
    assert n_strips % ring == 0 and A_LOOKAHEAD < ring and n_chunks % A_UNROLL == 0

    def issue_score(slot, chunk, h):
        k = k_ref[0, pl.ds(pl.multiple_of(chunk * A_TK, A_TK), A_TK), :]
        s_ref[slot] = _dot(k, qbd_ref[h])

    def body(c, carry):
        ms, accs = list(carry[0]), list(carry[1])
        next_base = jnp.minimum(c + 1, n_chunks // A_UNROLL - 1) * A_UNROLL
        for idx, (u, h) in enumerate(strips):
            nxt = idx + A_LOOKAHEAD
            if nxt < n_strips:
                issue_score(nxt % ring, c * A_UNROLL + strips[nxt][0], strips[nxt][1])
            else:
                u2, h2 = strips[nxt - n_strips]
                issue_score(nxt % ring, next_base + u2, h2)
            slot = idx % ring
            m_new = jnp.maximum(ms[h], jnp.max(s_ref[slot], axis=0, keepdims=True))
            alpha = jnp.exp2(ms[h] - m_new)
            p = jnp.exp2(s_ref[slot] - m_new).astype(BF16)
            pv = _dot(vt_ref[0, h // group, c * A_UNROLL + u], p)
            ms[h] = m_new
            accs[h] = accs[h] * alpha + pv
        return tuple(ms), tuple(accs)

    for idx in range(A_LOOKAHEAD):
        issue_score(idx, strips[idx][0], strips[idx][1])
    init = (tuple(jnp.full((1, tq), -jnp.inf, F32) for _ in range(A_Q_HEADS)),
            tuple(jnp.zeros((A_V_ROWS, tq), F32) for _ in range(A_Q_HEADS)))
    _, accs = lax.fori_loop(0, n_chunks // A_UNROLL, body, init)
    outs = []
    for h in range(A_Q_HEADS):
        a = accs[h]
        outs.append(a[0:HEAD_DIM] / a[HEAD_DIM:HEAD_DIM + 1])
    o_ref[0] = jnp.concatenate(outs, axis=0).T.astype(BF16)


def _attn_a(qa, ka, vt):
    nb, s, _ = qa.shape
    tq = A_TQ
    return pl.pallas_call(
        _attn_a_kernel,
        grid=(nb, s // tq),
        in_specs=[pl.BlockSpec((1, tq, A_Q_W), lambda b, i: (b, i, 0)),
                  _resident((1, s, A_KV_W), lambda b, i: (b, 0, 0)),
                  _resident((1,) + vt.shape[1:], lambda b, i: (b, 0, 0, 0, 0))],
        out_specs=pl.BlockSpec((1, tq, A_Q_W), lambda b, i: (b, i, 0)),
        out_shape=jax.ShapeDtypeStruct((nb, s, A_Q_W), BF16),
        scratch_shapes=[pltpu.VMEM((A_Q_HEADS, 2 * HEAD_DIM, tq), BF16),
                        pltpu.VMEM((A_SCORE_RING, A_TK, tq), F32)],
        compiler_params=_cparams(2),
        name="attn_a",
    )(qa, ka, vt)


def _left_lanes(shape):
    return lax.broadcasted_iota(jnp.int32, shape, 1) < HEAD_DIM


def _head_scores(q, kw, hd):
    left = _left_lanes(q.shape)
    zero = jnp.zeros_like(q)
    qh = jnp.where(left, q, zero) if hd == 0 else jnp.where(left, zero, q)
    return _dot_nt(qh, kw)


def _softmax_pv(s, vw):
    m = jnp.max(s, axis=-1, keepdims=True)
    p = jnp.exp(s - m)
    l = jnp.sum(p, axis=-1, keepdims=True)
    o = _dot(p.astype(BF16), vw) * (1.0 / l)
    return o, m + jnp.log(l)


def _run_tasks(tasks, lookahead, score, finish):
    pending = [score(t) for t in tasks[:lookahead]]
    for idx, t in enumerate(tasks):
        if idx + lookahead < len(tasks):
            pending.append(score(tasks[idx + lookahead]))
        finish(t, pending.pop(0))


def _attn_b_kernel(*refs):
    n_g = len(B_GROUPS)
    q_refs, k_refs, v_refs = refs[0:n_g], refs[n_g:2 * n_g], refs[2 * n_g:3 * n_g]
    bias_ref, o_ref = refs[3 * n_g], refs[3 * n_g + 1]
    o_scr, l_scr = refs[3 * n_g + 2:4 * n_g + 2], refs[4 * n_g + 2:5 * n_g + 2]
    tile_i = pl.program_id(2)
    left = _left_lanes((B_TQ, LANES))
    n_sub = B_TILE // B_TQ
    win = B_TQ + 2 * B_HALF

    def body(it, carry):
        geo = {}
        for j in range(B_SUBS_PER_ITER):
            k = it * B_SUBS_PER_ITER + j
            for g, (_, dil) in enumerate(B_GROUPS):
                per_res = n_sub // dil
                shift = per_res.bit_length() - 1
                rho = lax.shift_right_logical(k, jnp.int32(shift))
                sub = k & (per_res - 1)
                u0 = tile_i * (B_TILE // dil) + sub * B_TQ
                ws = jnp.clip(u0 - B_HALF, 0, k_refs[g].shape[2] - win)
                var = 1 - lax.shift_right_arithmetic(ws - (u0 - B_HALF), jnp.int32(B_HALF.bit_length() - 1))
                geo[(j, g)] = (rho, sub, pl.multiple_of(ws, B_HALF), var, dil)

        def score(task):
            j, g, hd = task
            rho, sub, ws, var, _ = geo[(j, g)]
            q = q_refs[g][0, rho, pl.ds(pl.multiple_of(sub * B_TQ, B_TQ), B_TQ), :]
            return _head_scores(q, k_refs[g][0, rho, pl.ds(ws, win), :], hd) + bias_ref[var]

        first = {}

        def finish(task, s):
            j, g, hd = task
            rho, sub, ws, _, dil = geo[(j, g)]
            m = jnp.max(s, axis=-1, keepdims=True)
            p = jnp.exp2(s - m)
            l = jnp.sum(p, axis=-1, keepdims=True)
            o = _dot(p.astype(BF16), v_refs[g][0, rho, pl.ds(ws, win), :]) * (1.0 / l)
            lse = m + jnp.log2(l)
            if hd == 0:
                first[(j, g)] = (o, lse)
                return
            o0, lse0 = first.pop((j, g))
            row0 = rho + (dil * B_TQ) * sub
            rows = pl.ds(row0, B_TQ, stride=dil) if dil > 1 else pl.ds(pl.multiple_of(row0, B_TQ), B_TQ)
            o_scr[g][rows, :] = jnp.where(left, o0, o)
            l_scr[g][rows, :] = jnp.where(left, lse0, lse)

        tasks = [(j, g, hd) for j in range(B_SUBS_PER_ITER) for g in range(n_g) for hd in range(2)]
        _run_tasks(tasks, B_LOOKAHEAD, score, finish)
        return carry

    lax.fori_loop(0, n_sub // B_SUBS_PER_ITER, body, 0)
    lses = [r[...] for r in l_scr]
    m = functools.reduce(jnp.maximum, lses)
    es = [jnp.exp2(l - m) for l in lses]
    num = functools.reduce(lambda a, b: a + b, [e * r[...] for e, r in zip(es, o_scr)])
    o_ref[0] = (num / functools.reduce(lambda a, b: a + b, es)).astype(BF16)


def _attn_b(groups):
    nb, _, s, _ = groups[0][0].shape
    assert s % B_TILE == 0 and all(s // dil >= B_TQ + 2 * B_HALF for _, dil in B_GROUPS)
    assert all(window // (2 * dil) == B_HALF and (B_TILE // B_TQ) % dil == 0 for window, dil in B_GROUPS)
    win = B_TQ + 2 * B_HALF
    shift = jnp.asarray([B_HALF, 0, -B_HALF], jnp.int32)[:, None, None]
    du = (jnp.arange(win, dtype=jnp.int32)[None, None, :] - jnp.arange(B_TQ, dtype=jnp.int32)[None, :, None]
          + shift - B_HALF)
    bias = jnp.where(jnp.abs(du) <= B_HALF, 0.0, NEG_BIG).astype(F32)
    q_specs, kv_specs = [], []
    for _, dil in B_GROUPS:
        q_specs.append(pl.BlockSpec((1, dil, B_TILE // dil, LANES), lambda b, hp, i: (b, 0, i, hp)))
        kv_specs.append(_resident((1, dil, s // dil, LANES), lambda b, hp, i: (b, 0, 0, hp)))
    qs, ks, vs = zip(*groups)
    return pl.pallas_call(
        _attn_b_kernel,
        grid=(nb, B_GROUP_W // LANES, s // B_TILE),
        in_specs=q_specs + kv_specs + kv_specs + [_resident(bias.shape, lambda b, hp, i: (0, 0, 0))],
        out_specs=pl.BlockSpec((1, B_TILE, LANES), lambda b, hp, i: (b, i, hp)),
        out_shape=jax.ShapeDtypeStruct((nb, s, B_GROUP_W), BF16),
        scratch_shapes=[pltpu.VMEM((B_TILE, LANES), F32) for _ in range(2 * len(B_GROUPS))],
        compiler_params=_cparams(3),
        name="attn_b",
    )(*qs, *ks, *vs, bias)


def _finish(y, x, g_post, gate):
    return x + gate * (_rms(y) * g_post)


def _out0_kernel(oa_ref, ob_ref, wa_ref, wb_ref, x_ref, g_ref, gate_ref, y_ref):
    y = _dot(oa_ref[0], wa_ref[...]) + _dot(ob_ref[0], wb_ref[...])
    y_ref[0] = _finish(y, x_ref[0], g_ref[...], gate_ref[0])


def _out0(oa, ob, w_out, x, g_post, gate):
    nb, s, d = x.shape
    tm = TOKEN_TILE
    tok = lambda b, i: (b, i, 0)
    wa, wb = w_out[:A_Q_W], w_out[A_Q_W:]
    return pl.pallas_call(
        _out0_kernel,
        grid=(nb, s // tm),
        in_specs=[pl.BlockSpec((1, tm, A_Q_W), tok),
                  pl.BlockSpec((1, tm, B_GROUP_W), tok),
                  _resident(wa.shape, lambda b, i: (0, 0)),
                  _resident(wb.shape, lambda b, i: (0, 0)),
                  pl.BlockSpec((1, tm, d), tok),
                  pl.BlockSpec((1, d), lambda b, i: (0, 0)),
                  pl.BlockSpec((1, 1, d), lambda b, i: (b, 0, 0))],
        out_specs=pl.BlockSpec((1, tm, d), tok),
        out_shape=jax.ShapeDtypeStruct(x.shape, F32),
        compiler_params=_cparams(2),
        name="out0",
    )(oa, ob, wa, wb, x, g_post, gate)


def _out1_kernel(o_ref, w_ref, x_ref, g_ref, gate_ref, y_ref):
    y_ref[0] = _finish(_dot(o_ref[0], w_ref[...]), x_ref[0], g_ref[...], gate_ref[0])


def _out1(o, w_out, x, g_post, gate):
    nb, s, d = x.shape
    tm = TOKEN_TILE
    tok = lambda b, i: (b, i, 0)
    return pl.pallas_call(
        _out1_kernel,
        grid=(nb, s // tm),
        in_specs=[pl.BlockSpec((1, tm, o.shape[-1]), tok),
                  _resident(w_out.shape, lambda b, i: (0, 0)),
                  pl.BlockSpec((1, tm, d), tok),
                  pl.BlockSpec((1, d), lambda b, i: (0, 0)),
                  pl.BlockSpec((1, 1, d), lambda b, i: (b, 0, 0))],
        out_specs=pl.BlockSpec((1, tm, d), tok),
        out_shape=jax.ShapeDtypeStruct(x.shape, F32),
        compiler_params=_cparams(2),
        name="out1",
    )(o, w_out, x, g_post, gate)


def _mlp_kernel(x_ref, gpre_ref, sh_ref, sc_ref, wu_ref, wd_ref, gpost_ref, gate_ref, y_ref):
    x = x_ref[0]
    hb = (_rms(x) * gpre_ref[...] * (1.0 + sc_ref[0]) + sh_ref[0]).astype(BF16)
    d = x.shape[-1]
    acc = jnp.zeros(x.shape, F32)
    for j in range(wu_ref.shape[1] // d):
        u = jnp.maximum(_dot(hb, wu_ref[:, j * d:(j + 1) * d]), 0.0)
        acc = acc + _dot((u * u).astype(BF16), wd_ref[j * d:(j + 1) * d, :])
    y_ref[0] = _finish(acc, x, gpost_ref[...], gate_ref[0])


def _mlp(x, g_pre, shift, scale, w_up, w_down, g_post, gate):
    nb, s, d = x.shape
    tm = TOKEN_TILE
    tok = lambda b, i: (b, i, 0)
    vec = pl.BlockSpec((1, d), lambda b, i: (0, 0))
    mod = pl.BlockSpec((1, 1, d), lambda b, i: (b, 0, 0))
    return pl.pallas_call(
        _mlp_kernel,
        grid=(nb, s // tm),
        in_specs=[pl.BlockSpec((1, tm, d), tok), vec, mod, mod,
                  _resident(w_up.shape, lambda b, i: (0, 0)),
                  _resident(w_down.shape, lambda b, i: (0, 0)),
                  vec, mod],
        out_specs=pl.BlockSpec((1, tm, d), tok),
        out_shape=jax.ShapeDtypeStruct(x.shape, F32),
        compiler_params=_cparams(2),
        name="mlp",
    )(x, g_pre, shift, scale, w_up, w_down, g_post, gate)


def _proj1_kernel(x_ref, g_ref, sh_ref, sc_ref, w_ref, q_ref, k_ref, v_ref):
    hb = (_rms(x_ref[0]) * g_ref[...] * (1.0 + sc_ref[0]) + sh_ref[0]).astype(BF16)
    d = q_ref.shape[-1]
    q_ref[0] = (_dot(hb, w_ref[:, 0:d]) * QK_SCALE).astype(BF16)
    k_ref[0] = _dot(hb, w_ref[:, d:2 * d]).astype(BF16)
    v_ref[0] = _dot(hb, w_ref[:, 2 * d:3 * d]).astype(BF16)


def _proj1(x, g_pre, shift, scale, w_in):
    nb, s, d = x.shape
    tm = TOKEN_TILE
    n = w_in.shape[1] // 3
    tok = lambda b, i: (b, i, 0)
    mod = pl.BlockSpec((1, 1, d), lambda b, i: (b, 0, 0))
    out = jax.ShapeDtypeStruct((nb, s, n), BF16)
    return pl.pallas_call(
        _proj1_kernel,
        grid=(nb, s // tm),
        in_specs=[pl.BlockSpec((1, tm, d), tok),
                  pl.BlockSpec((1, d), lambda b, i: (0, 0)), mod, mod,
                  _resident(w_in.shape, lambda b, i: (0, 0))],
        out_specs=(pl.BlockSpec((1, tm, n), tok),) * 3,
        out_shape=(out, out, out),
        compiler_params=_cparams(2),
        name="proj1",
    )(x, g_pre, shift, scale, w_in)


def _attn_c_kernel(q_ref, k_ref, v_ref, bias_ref, o_ref):
    rows = k_ref.shape[1] // GRID_W
    pair_tokens = 2 * GRID_W
    win = C_WIN_ROWS * GRID_W
    blk = pl.program_id(2)
    left = _left_lanes((pair_tokens, LANES))
    starts, variants = [], []
    for j in range(C_PAIRS):
        nominal = 2 * (blk * C_PAIRS + j) - NA_ROWS // 2
        ws = jnp.clip(nominal, 0, rows - C_WIN_ROWS)
        variants.append(2 - lax.shift_right_arithmetic(ws - nominal, 1))
        starts.append(pl.multiple_of(ws * GRID_W, GRID_W))
    toks = lambda j: slice(j * pair_tokens, (j + 1) * pair_tokens)

    def score(task):
        j, hd = task
        return _head_scores(q_ref[0, toks(j), :], k_ref[0, pl.ds(starts[j], win), :], hd) + bias_ref[hd, variants[j]]

    first = {}

    def finish(task, s):
        j, hd = task
        o, _ = _softmax_pv(s, v_ref[0, pl.ds(starts[j], win), :])
        if hd == 0:
            first[j] = o
        else:
            o_ref[0, toks(j), :] = jnp.where(left, first.pop(j), o).astype(BF16)

    _run_tasks([(j, hd) for j in range(C_PAIRS) for hd in range(2)], C_LOOKAHEAD, score, finish)


def _na_bias_table(rpb, rows):
    n_heads, n_dr, n_dc = rpb.shape
    n_pairs = rows // 2
    c = np.arange(GRID_W)
    cs = np.clip(c - NA_COLS // 2, 0, GRID_W - NA_COLS)
    kc = np.arange(GRID_W)
    valid_c = (kc[None, :] >= cs[:, None]) & (kc[None, :] < cs[:, None] + NA_COLS)
    span = 2 * GRID_W - 1
    side = (span - n_dc) // 2
    g = jnp.pad(rpb.astype(F32), ((0, 0), (0, 0), (side, side)), constant_values=NEG_BIG)
    flat = jnp.tile(g, (1, 1, GRID_W))
    toep = flat[:, :, GRID_W - 1:GRID_W - 1 + GRID_W * (span - 1)]
    toep = toep.reshape(n_heads, n_dr, GRID_W, span - 1)[..., :GRID_W]
    toep = jnp.where(valid_c[None, None], toep, NEG_BIG)
    masked = jnp.full((n_heads, GRID_W, GRID_W), NEG_BIG, F32)
    tiles = []
    for i in (0, 1, 2, n_pairs - 2, n_pairs - 1):
        ws = int(np.clip(2 * i - NA_ROWS // 2, 0, rows - C_WIN_ROWS))
        halves = []
        for qr in range(2):
            r = 2 * i + qr
            rs = int(np.clip(r - NA_ROWS // 2, 0, rows - NA_ROWS))
            blocks = []
            for kr in range(ws, ws + C_WIN_ROWS):
                blocks.append(toep[:, kr - r + NA_ROWS - 1] if rs <= kr < rs + NA_ROWS else masked)
            halves.append(jnp.concatenate(blocks, axis=-1))
        tiles.append(jnp.concatenate(halves, axis=1))
    return jnp.stack(tiles, axis=1)


def _attn_c(q, k, v, bias):
    nb, s, w = q.shape
    pairs = w // LANES
    step_tokens = C_PAIRS * 2 * GRID_W
    n_var, tq, win = bias.shape[1:]
    col = lambda b, hp, i: (b, 0, hp)
    return pl.pallas_call(
        _attn_c_kernel,
        grid=(nb, pairs, s // step_tokens),
        in_specs=[pl.BlockSpec((1, step_tokens, LANES), lambda b, hp, i: (b, i, hp)),
                  _resident((1, s, LANES), col),
                  _resident((1, s, LANES), col),
                  _resident((2, n_var, tq, win), lambda b, hp, i: (hp, 0, 0, 0))],
        out_specs=pl.BlockSpec((1, step_tokens, LANES), lambda b, hp, i: (b, i, hp)),
        out_shape=jax.ShapeDtypeStruct((nb, s, w), BF16),
        compiler_params=_cparams(3),
        name="attn_c",
    )(q, k, v, bias)


def _rope_parts(s, tm):
    def cos_sin(pos, dim):
        inv_freq = ROPE_THETA ** (-jnp.arange(0, dim, 2, dtype=F32) / dim)
        ang = pos.astype(F32)[:, None] * inv_freq[None, :]
        return jnp.cos(ang), jnp.sin(ang)

    rows = s // GRID_W
    cr, sr = cos_sin(jnp.arange(rows, dtype=jnp.int32), HEAD_DIM // 2)
    cc, sc = cos_sin(jnp.arange(GRID_W, dtype=jnp.int32), HEAD_DIM // 2)
    zr, zc = jnp.zeros_like(cr), jnp.zeros_like(cc)
    axial = (jnp.concatenate([cr, cr, zr, zr] * 2, axis=-1), jnp.concatenate([-sr, sr, zr, zr] * 2, axis=-1),
             jnp.concatenate([zc, zc, cc, cc] * 2, axis=-1), jnp.concatenate([zc, zc, -sc, sc] * 2, axis=-1))
    c0, s0 = cos_sin(jnp.arange(0, s, tm, dtype=jnp.int32), HEAD_DIM)
    cj, sj = cos_sin(jnp.arange(tm, dtype=jnp.int32), HEAD_DIM)
    wide = lambda a: jnp.concatenate([a] * 4, axis=-1)
    linear = (wide(c0)[:, None, :], wide(s0)[:, None, :], wide(cj), wide(sj))
    return axial + linear


def _head_sum_matrix(n):
    idx = np.arange(n) // HEAD_DIM
    return jnp.asarray(idx[:, None] == idx[None, :], BF16)


def kernel(x, c, ada_w, ada_b, norm_g, ab_w_in, ab_w_out, a_q_gain, a_k_gain,
           c_w_in, c_w_out, c_rpb, mlp_w_up, mlp_w_down):
    nb, s, d = x.shape
    assert s % (C_PAIRS * 2 * GRID_W) == 0 and s // GRID_W >= 2 * C_WIN_ROWS
    mods = _adaln(c, ada_w, ada_b).reshape(ada_w.shape[0], 2, nb, 3, d)

    def mod(layer, which):
        m = mods[layer, which]
        return m[:, 0:1], m[:, 1:2], m[:, 2:3]

    ropes = _rope_parts(s, TOKEN_TILE)
    gain_qk = jnp.concatenate([jnp.tile(a_q_gain[0] * (QK_SCALE * LOG2_E), A_Q_HEADS),
                               jnp.tile(a_k_gain[0], A_KV_HEADS)])[None, :]
    gmat = _head_sum_matrix(A_Q_W + A_KV_W)

    shift, scale, gate = mod(0, 0)
    qa, ka, vt, groups = _proj0(x, norm_g[0, 0:1], shift, scale, ab_w_in[0].astype(BF16), ropes, gain_qk, gmat)
    oa = _attn_a(qa, ka, vt)
    ob = _attn_b(groups)
    x = _out0(oa, ob, ab_w_out[0].astype(BF16), x, norm_g[0, 1:2], gate)
    shift, scale, gate = mod(0, 1)
    x = _mlp(x, norm_g[0, 2:3], shift, scale, mlp_w_up[0].astype(BF16), mlp_w_down[0].astype(BF16),
             norm_g[0, 3:4], gate)

    shift, scale, gate = mod(1, 0)
    q, k, v = _proj1(x, norm_g[1, 0:1], shift, scale, c_w_in[0].astype(BF16))
    oc = _attn_c(q, k, v, _na_bias_table(c_rpb[0], s // GRID_W))
    x = _out1(oc, c_w_out[0].astype(BF16), x, norm_g[1, 1:2], gate)
    shift, scale, gate = mod(1, 1)
    x = _mlp(x, norm_g[1, 2:3], shift, scale, mlp_w_up[1].astype(BF16), mlp_w_down[1].astype(BF16),
             norm_g[1, 3:4], gate)
    return x
```

```python
import functools
import math

import numpy as np
import jax
import jax.numpy as jnp
from jax import lax
from jax.experimental import pallas as pl
from jax.experimental.pallas import tpu as pltpu

F32 = jnp.float32
BF16 = jnp.bfloat16

HEAD_DIM = 64
GRID_W = 64
ROPE_THETA = 10000.0
RMS_EPS = 1e-6
A_Q_HEADS = 8
A_KV_HEADS = 2
B_GROUPS = ((128, 1), (512, 4), (2048, 16))
B_HEADS_PER_GROUP = 4
C_HEADS = 16
NA_ROWS = 8
NA_COLS = 16
A_Q_W = A_Q_HEADS * HEAD_DIM
A_KV_W = A_KV_HEADS * HEAD_DIM
B_GROUP_W = B_HEADS_PER_GROUP * HEAD_DIM
B_W = len(B_GROUPS) * B_GROUP_W
AB_A_W = A_Q_W + 2 * A_KV_W
QK_SCALE = 1.0 / math.sqrt(HEAD_DIM)
LOG2_E = math.log2(math.e)

LANES = 128
V7X_VMEM_LIMIT_BYTES = 56 * 1024 * 1024
NEG_BIG = -1e30

TOKEN_TILE = 512
A_TQ = 256
A_TK = 256
A_UNROLL = 16
A_LOOKAHEAD = 8
A_SCORE_RING = 16
A_V_ROWS = 80
B_TILE = 2048
B_TQ = 128
B_HALF = 64
B_SUBS_PER_ITER = 4
B_LOOKAHEAD = 4
C_LOOKAHEAD = 3
C_PAIRS = 8
C_WIN_ROWS = 10


def _cparams(n_axes):
    return pltpu.CompilerParams(dimension_semantics=("arbitrary",) * n_axes,
                                vmem_limit_bytes=V7X_VMEM_LIMIT_BYTES)


def _resident(block_shape, index_map):
    return pl.BlockSpec(block_shape, index_map, pipeline_mode=pl.Buffered(1))


def _rms(x):
    return x * lax.rsqrt(jnp.mean(x * x, axis=-1, keepdims=True) + RMS_EPS)


def _dot(a, b):
    return jnp.dot(a, b, preferred_element_type=F32)


def _dot_nt(a, b):
    return lax.dot_general(a, b, (((1,), (1,)), ((), ())), preferred_element_type=F32)


def _rope_slab(x, cos, sin_signed, half):
    lane = lax.broadcasted_iota(jnp.int32, x.shape, 1)
    first = (lane & half) == 0
    rot = jnp.where(first, pltpu.roll(x, LANES - half, 1), pltpu.roll(x, half, 1))
    return x * cos + rot * sin_signed


def _adaln_kernel(ct_ref, w_ref, b_ref, o_ref):
    ct = ct_ref[...]
    cond = ct * jax.nn.sigmoid(ct)
    w = w_ref[0]
    rows = [jnp.sum(w * cond[:, b:b + 1], axis=0, keepdims=True) for b in range(ct.shape[1])]
    o_ref[0] = jnp.concatenate(rows, axis=0) + b_ref[0]


def _adaln(c, ada_w, ada_b):
    nb, d = c.shape
    n_mod = ada_w.shape[0] * ada_w.shape[1]
    n_out = ada_w.shape[-1]
    tn = 768
    w = ada_w.reshape(n_mod, d, n_out)
    b = ada_b.reshape(n_mod, 1, n_out)
    return pl.pallas_call(
        _adaln_kernel,
        grid=(n_mod, n_out // tn),
        in_specs=[pl.BlockSpec((d, nb), lambda m, j: (0, 0)),
                  pl.BlockSpec((1, d, tn), lambda m, j: (m, 0, j)),
                  pl.BlockSpec((1, 1, tn), lambda m, j: (m, 0, j))],
        out_specs=pl.BlockSpec((1, nb, tn), lambda m, j: (m, 0, j)),
        out_shape=jax.ShapeDtypeStruct((n_mod, nb, n_out), F32),
        compiler_params=_cparams(2),
        name="adaln",
    )(c.T, w, b)


def _proj0_kernel(x_ref, g_ref, sh_ref, sc_ref, w_ref, rc_ref, rs_ref, cc_ref, cs_ref, c0_ref, s0_ref, cj_ref, sj_ref,
                  gain_ref, gmat_ref, qa_ref, ka_ref, vt_ref, *rest):
    b_refs, stage_ref = rest[:-1], rest[-1]
    tm = x_ref.shape[1]
    col_c, col_s = cc_ref[...], cs_ref[...]
    ca = jnp.concatenate([rc_ref[r:r + 1, :] + col_c for r in range(tm // GRID_W)], axis=0)
    sa = jnp.concatenate([rs_ref[r:r + 1, :] + col_s for r in range(tm // GRID_W)], axis=0)
    c0, s0, cj, sj = c0_ref[0], s0_ref[0], cj_ref[...], sj_ref[...]
    lane = lax.broadcasted_iota(jnp.int32, (tm, LANES), 1)
    cb = c0 * cj - s0 * sj
    sb = jnp.where((lane & (HEAD_DIM // 2)) == 0, -1.0, 1.0) * (s0 * cj + c0 * sj)
    x = x_ref[0]
    h = _rms(x) * g_ref[...] * (1.0 + sc_ref[0]) + sh_ref[0]
    hb = h.astype(BF16)

    pa = _dot(hb, w_ref[:, 0:AB_A_W])
    n_qk = A_Q_W + A_KV_W
    qk = pa[:, 0:n_qk]
    sq = qk * qk
    hi = sq.astype(BF16)
    lo = (sq - hi.astype(F32)).astype(BF16)
    ms = (_dot(hi, gmat_ref[...]) + _dot(lo, gmat_ref[...])) * (1.0 / HEAD_DIM)
    qk = qk * lax.rsqrt(ms + RMS_EPS) * gain_ref[...]
    for j in range(n_qk // LANES):
        slab = _rope_slab(qk[:, j * LANES:(j + 1) * LANES], ca, sa, HEAD_DIM // 4).astype(BF16)
        if j < A_Q_W // LANES:
            qa_ref[0, :, j * LANES:(j + 1) * LANES] = slab
        else:
            ka_ref[0] = slab
    vt = pa[:, n_qk:AB_A_W].T
    ones = jnp.ones((A_V_ROWS - HEAD_DIM, A_TK), BF16)
    for kvh in range(A_KV_HEADS):
        for c in range(tm // A_TK):
            vt_ref[0, kvh, c, 0:HEAD_DIM, :] = vt[kvh * HEAD_DIM:(kvh + 1) * HEAD_DIM,
                                                  c * A_TK:(c + 1) * A_TK].astype(BF16)
            vt_ref[0, kvh, c, HEAD_DIM:A_V_ROWS, :] = ones

    pb = _dot(hb, w_ref[:, AB_A_W:AB_A_W + 3 * B_W])
    slot = 0
    for kind in range(3):
        for g, (_, dil) in enumerate(B_GROUPS):
            out_ref = b_refs[3 * g + kind]
            for hp in range(B_GROUP_W // LANES):
                lanes = slice(hp * LANES, (hp + 1) * LANES)
                col = kind * B_W + g * B_GROUP_W + hp * LANES
                slab = pb[:, col:col + LANES]
                if kind == 0:
                    slab = _rope_slab(slab, cb, sb, HEAD_DIM // 2) * (QK_SCALE * LOG2_E)
                elif kind == 1:
                    slab = _rope_slab(slab, cb, sb, HEAD_DIM // 2)
                if dil == 1:
                    out_ref[0, 0, :, lanes] = slab.astype(BF16)
                    continue
                stage_ref[slot] = slab
                for rho in range(dil):
                    out_ref[0, rho, :, lanes] = stage_ref[slot, pl.ds(rho, tm // dil, stride=dil), :].astype(BF16)
                slot += 1


def _proj0(x, g_pre, shift, scale, w_in, ropes, gain_qk, gmat):
    nb, s, d = x.shape
    tm = TOKEN_TILE
    tok = lambda b, i: (b, i, 0)
    mod = lambda b, i: (b, 0, 0)
    row_spec = pl.BlockSpec((tm // GRID_W, LANES), lambda b, i: (i, 0))
    col_spec = pl.BlockSpec((GRID_W, LANES), lambda b, i: (0, 0))
    base_spec = pl.BlockSpec((1, 1, LANES), lambda b, i: (i, 0, 0))
    in_tile_spec = pl.BlockSpec((tm, LANES), lambda b, i: (0, 0))
    n_qk = A_Q_W + A_KV_W
    out_shape = [
        jax.ShapeDtypeStruct((nb, s, A_Q_W), BF16),
        jax.ShapeDtypeStruct((nb, s, A_KV_W), BF16),
        jax.ShapeDtypeStruct((nb, A_KV_HEADS, s // A_TK, A_V_ROWS, A_TK), BF16),
    ]
    out_specs = [
        pl.BlockSpec((1, tm, A_Q_W), tok),
        pl.BlockSpec((1, tm, A_KV_W), tok),
        pl.BlockSpec((1, A_KV_HEADS, tm // A_TK, A_V_ROWS, A_TK), lambda b, i: (b, 0, i, 0, 0)),
    ]
    n_staged = 0
    for _, dil in B_GROUPS:
        for _ in range(3):
            out_shape.append(jax.ShapeDtypeStruct((nb, dil, s // dil, B_GROUP_W), BF16))
            out_specs.append(pl.BlockSpec((1, dil, tm // dil, B_GROUP_W), lambda b, i: (b, 0, i, 0)))
        n_staged += 3 * (B_GROUP_W // LANES) * (dil > 1)
    outs = pl.pallas_call(
        _proj0_kernel,
        grid=(nb, s // tm),
        in_specs=[pl.BlockSpec((1, tm, d), tok),
                  pl.BlockSpec((1, d), lambda b, i: (0, 0)),
                  pl.BlockSpec((1, 1, d), mod),
                  pl.BlockSpec((1, 1, d), mod),
                  _resident(w_in.shape, lambda b, i: (0, 0)),
                  row_spec, row_spec, col_spec, col_spec, base_spec, base_spec, in_tile_spec, in_tile_spec,
                  pl.BlockSpec((1, n_qk), lambda b, i: (0, 0)),
                  _resident((n_qk, n_qk), lambda b, i: (0, 0))],
        out_specs=out_specs,
        out_shape=out_shape,
        scratch_shapes=[pltpu.VMEM((n_staged, tm, LANES), F32)],
        compiler_params=_cparams(2),
        name="proj0",
    )(x, g_pre, shift, scale, w_in, *ropes, gain_qk, gmat)
    qa, ka, vt = outs[:3]
    groups = [tuple(outs[3 + 3 * g:6 + 3 * g]) for g in range(len(B_GROUPS))]
    return qa, ka, vt, groups


def _attn_a_kernel(q_ref, k_ref, vt_ref, o_ref, qbd_ref, s_ref):
    tq = q_ref.shape[1]
    n_chunks = k_ref.shape[1] // A_TK
    qt = q_ref[0].astype(F32).T
    zeros = jnp.zeros((HEAD_DIM, tq), F32)
    group = A_Q_HEADS // A_KV_HEADS
    for h in range(A_Q_HEADS):
        blk = qt[h * HEAD_DIM:(h + 1) * HEAD_DIM]
        full = jnp.concatenate([blk, zeros] if h < group else [zeros, blk], axis=0)
        qbd_ref[h] = full.astype(BF16)

    strips = [(u, h) for u in range(A_UNROLL) for h in range(A_Q_HEADS)]
    n_strips = len(strips)
    ring = s_ref.shape[0]
    assert n_strips % ring == 0 and A_LOOKAHEAD < ring and n_chunks % A_UNROLL == 0

    def issue_score(slot, chunk, h):
        k = k_ref[0, pl.ds(pl.multiple_of(chunk * A_TK, A_TK), A_TK), :]
        s_ref[slot] = _dot(k, qbd_ref[h])

    def body(c, carry):
        ms, accs = list(carry[0]), list(carry[1])
        next_base = jnp.minimum(c + 1, n_chunks // A_UNROLL - 1) * A_UNROLL
        for idx, (u, h) in enumerate(strips):
            nxt = idx + A_LOOKAHEAD
            if nxt < n_strips:
                issue_score(nxt % ring, c * A_UNROLL + strips[nxt][0], strips[nxt][1])
            else:
                u2, h2 = strips[nxt - n_strips]
                issue_score(nxt % ring, next_base + u2, h2)
            slot = idx % ring
            m_new = jnp.maximum(ms[h], jnp.max(s_ref[slot], axis=0, keepdims=True))
            alpha = jnp.exp2(ms[h] - m_new)
            p = jnp.exp2(s_ref[slot] - m_new).astype(BF16)
            pv = _dot(vt_ref[0, h // group, c * A_UNROLL + u], p)
            ms[h] = m_new
            accs[h] = accs[h] * alpha + pv
        return tuple(ms), tuple(accs)

    for idx in range(A_LOOKAHEAD):
        issue_score(idx, strips[idx][0], strips[idx][1])
    init = (tuple(jnp.full((1, tq), -jnp.inf, F32) for _ in range(A_Q_HEADS)),
            tuple(jnp.zeros((A_V_ROWS, tq), F32) for _ in range(A_Q_HEADS)))
    _, accs = lax.fori_loop(0, n_chunks // A_UNROLL, body, init)
    outs = []
    for h in range(A_Q_HEADS):
        a = accs[h]
        outs.append(a[0:HEAD_DIM] / a[HEAD_DIM:HEAD_DIM + 1])
    o_ref[0] = jnp.concatenate(outs, axis=0).T.astype(BF16)


def _attn_a(qa, ka, vt):
    nb, s, _ = qa.shape
    tq = A_TQ
    return pl.pallas_call(
        _attn_a_kernel,
        grid=(nb, s // tq),
        in_specs=[pl.BlockSpec((1, tq, A_Q_W), lambda b, i: (b, i, 0)),
                  _resident((1, s, A_KV_W), lambda b, i: (b, 0, 0)),
                  _resident((1,) + vt.shape[1:], lambda b, i: (b, 0, 0, 0, 0))],
        out_specs=pl.BlockSpec((1, tq, A_Q_W), lambda b, i: (b, i, 0)),
        out_shape=jax.ShapeDtypeStruct((nb, s, A_Q_W), BF16),
        scratch_shapes=[pltpu.VMEM((A_Q_HEADS, 2 * HEAD_DIM, tq), BF16),
                        pltpu.VMEM((A_SCORE_RING, A_TK, tq), F32)],
        compiler_params=_cparams(2),
        name="attn_a",
    )(qa, ka, vt)


def _left_lanes(shape):
    return lax.broadcasted_iota(jnp.int32, shape, 1) < HEAD_DIM


def _head_scores(q, kw, hd):
    left = _left_lanes(q.shape)
    zero = jnp.zeros_like(q)
    qh = jnp.where(left, q, zero) if hd == 0 else jnp.where(left, zero, q)
    return _dot_nt(qh, kw)


def _softmax_pv(s, vw):
    m = jnp.max(s, axis=-1, keepdims=True)
    p = jnp.exp(s - m)
    l = jnp.sum(p, axis=-1, keepdims=True)
    o = _dot(p.astype(BF16), vw) * (1.0 / l)
    return o, m + jnp.log(l)


def _run_tasks(tasks, lookahead, score, finish):
    pending = [score(t) for t in tasks[:lookahead]]
    for idx, t in enumerate(tasks):
        if idx + lookahead < len(tasks):
            pending.append(score(tasks[idx + lookahead]))
        finish(t, pending.pop(0))


def _attn_b_kernel(*refs):
    n_g = len(B_GROUPS)
    q_refs, k_refs, v_refs = refs[0:n_g], refs[n_g:2 * n_g], refs[2 * n_g:3 * n_g]
    bias_ref, o_ref = refs[3 * n_g], refs[3 * n_g + 1]
    o_scr, l_scr = refs[3 * n_g + 2:4 * n_g + 2], refs[4 * n_g + 2:5 * n_g + 2]
    tile_i = pl.program_id(2)
    left = _left_lanes((B_TQ, LANES))
    n_sub = B_TILE // B_TQ
    win = B_TQ + 2 * B_HALF

    def body(it, carry):
        geo = {}
        for j in range(B_SUBS_PER_ITER):
            k = it * B_SUBS_PER_ITER + j
            for g, (_, dil) in enumerate(B_GROUPS):
                per_res = n_sub // dil
                shift = per_res.bit_length() - 1
                rho = lax.shift_right_logical(k, jnp.int32(shift))
                sub = k & (per_res - 1)
                u0 = tile_i * (B_TILE // dil) + sub * B_TQ
                ws = jnp.clip(u0 - B_HALF, 0, k_refs[g].shape[2] - win)
                var = 1 - lax.shift_right_arithmetic(ws - (u0 - B_HALF), jnp.int32(B_HALF.bit_length() - 1))
                geo[(j, g)] = (rho, sub, pl.multiple_of(ws, B_HALF), var, dil)

        def score(task):
            j, g, hd = task
            rho, sub, ws, var, _ = geo[(j, g)]
            q = q_refs[g][0, rho, pl.ds(pl.multiple_of(sub * B_TQ, B_TQ), B_TQ), :]
            return _head_scores(q, k_refs[g][0, rho, pl.ds(ws, win), :], hd) + bias_ref[var]

        first = {}

        def finish(task, s):
            j, g, hd = task
            rho, sub, ws, _, dil = geo[(j, g)]
            m = jnp.max(s, axis=-1, keepdims=True)
            p = jnp.exp2(s - m)
            l = jnp.sum(p, axis=-1, keepdims=True)
            o = _dot(p.astype(BF16), v_refs[g][0, rho, pl.ds(ws, win), :]) * (1.0 / l)
            lse = m + jnp.log2(l)
            if hd == 0:
                first[(j, g)] = (o, lse)
                return
            o0, lse0 = first.pop((j, g))
            row0 = rho + (dil * B_TQ) * sub
            rows = pl.ds(row0, B_TQ, stride=dil) if dil > 1 else pl.ds(pl.multiple_of(row0, B_TQ), B_TQ)
            o_scr[g][rows, :] = jnp.where(left, o0, o)
            l_scr[g][rows, :] = jnp.where(left, lse0, lse)

        tasks = [(j, g, hd) for j in range(B_SUBS_PER_ITER) for g in range(n_g) for hd in range(2)]
        _run_tasks(tasks, B_LOOKAHEAD, score, finish)
        return carry

    lax.fori_loop(0, n_sub // B_SUBS_PER_ITER, body, 0)
    lses = [r[...] for r in l_scr]
    m = functools.reduce(jnp.maximum, lses)
    es = [jnp.exp2(l - m) for l in lses]
    num = functools.reduce(lambda a, b: a + b, [e * r[...] for e, r in zip(es, o_scr)])
    o_ref[0] = (num / functools.reduce(lambda a, b: a + b, es)).astype(BF16)


def _attn_b(groups):
    nb, _, s, _ = groups[0][0].shape
    assert s % B_TILE == 0 and all(s // dil >= B_TQ + 2 * B_HALF for _, dil in B_GROUPS)
    assert all(window // (2 * dil) == B_HALF and (B_TILE // B_TQ) % dil == 0 for window, dil in B_GROUPS)
    win = B_TQ + 2 * B_HALF
    shift = jnp.asarray([B_HALF, 0, -B_HALF], jnp.int32)[:, None, None]
    du = (jnp.arange(win, dtype=jnp.int32)[None, None, :] - jnp.arange(B_TQ, dtype=jnp.int32)[None, :, None]
          + shift - B_HALF)
    bias = jnp.where(jnp.abs(du) <= B_HALF, 0.0, NEG_BIG).astype(F32)
    q_specs, kv_specs = [], []
    for _, dil in B_GROUPS:
        q_specs.append(pl.BlockSpec((1, dil, B_TILE // dil, LANES), lambda b, hp, i: (b, 0, i, hp)))
        kv_specs.append(_resident((1, dil, s // dil, LANES), lambda b, hp, i: (b, 0, 0, hp)))
    qs, ks, vs = zip(*groups)
    return pl.pallas_call(
        _attn_b_kernel,
        grid=(nb, B_GROUP_W // LANES, s // B_TILE),
        in_specs=q_specs + kv_specs + kv_specs + [_resident(bias.shape, lambda b, hp, i: (0, 0, 0))],
        out_specs=pl.BlockSpec((1, B_TILE, LANES), lambda b, hp, i: (b, i, hp)),
        out_shape=jax.ShapeDtypeStruct((nb, s, B_GROUP_W), BF16),
        scratch_shapes=[pltpu.VMEM((B_TILE, LANES), F32) for _ in range(2 * len(B_GROUPS))],
        compiler_params=_cparams(3),
        name="attn_b",
    )(*qs, *ks, *vs, bias)


def _finish(y, x, g_post, gate):
    return x + gate * (_rms(y) * g_post)


def _mix_mlp_kernel(*refs, n_parts):
    parts, weights = refs[:n_parts], refs[n_parts:2 * n_parts]
    (x_ref, gmix_ref, gate_mix_ref, gpre_ref, sh_ref, sc_ref, wu_ref, wd_ref, gpost_ref, gate_ref,
     y_ref) = refs[2 * n_parts:]
    y = _dot(parts[0][0], weights[0][...])
    for o_ref, w_ref in zip(parts[1:], weights[1:]):
        y = y + _dot(o_ref[0], w_ref[...])
    x = _finish(y, x_ref[0], gmix_ref[...], gate_mix_ref[0])
    hb = (_rms(x) * gpre_ref[...] * (1.0 + sc_ref[0]) + sh_ref[0]).astype(BF16)
    d = x.shape[-1]
    acc = jnp.zeros(x.shape, F32)
    for j in range(wu_ref.shape[1] // d):
        u = jnp.maximum(_dot(hb, wu_ref[:, j * d:(j + 1) * d]), 0.0)
        acc = acc + _dot((u * u).astype(BF16), wd_ref[j * d:(j + 1) * d, :])
    y_ref[0] = _finish(acc, x, gpost_ref[...], gate_ref[0])


def _mix_mlp(parts, w_outs, x, g_mix, gate_mix, g_pre, shift, scale, w_up, w_down, g_post, gate):
    nb, s, d = x.shape
    tm = TOKEN_TILE
    tok = lambda b, i: (b, i, 0)
    vec = pl.BlockSpec((1, d), lambda b, i: (0, 0))
    mod = pl.BlockSpec((1, 1, d), lambda b, i: (b, 0, 0))
    const = lambda a: _resident(a.shape, lambda b, i: (0, 0))
    return pl.pallas_call(
        functools.partial(_mix_mlp_kernel, n_parts=len(parts)),
        grid=(nb, s // tm),
        in_specs=[pl.BlockSpec((1, tm, p.shape[-1]), tok) for p in parts] + [const(w) for w in w_outs] +
                 [pl.BlockSpec((1, tm, d), tok), vec, mod, vec, mod, mod, const(w_up), const(w_down), vec, mod],
        out_specs=pl.BlockSpec((1, tm, d), tok),
        out_shape=jax.ShapeDtypeStruct(x.shape, F32),
        compiler_params=_cparams(2),
        name="mix_mlp",
    )(*parts, *w_outs, x, g_mix, gate_mix, g_pre, shift, scale, w_up, w_down, g_post, gate)


def _proj1_kernel(x_ref, g_ref, sh_ref, sc_ref, w_ref, q_ref, k_ref, v_ref):
    hb = (_rms(x_ref[0]) * g_ref[...] * (1.0 + sc_ref[0]) + sh_ref[0]).astype(BF16)
    d = q_ref.shape[-1]
    q_ref[0] = (_dot(hb, w_ref[:, 0:d]) * QK_SCALE).astype(BF16)
    k_ref[0] = _dot(hb, w_ref[:, d:2 * d]).astype(BF16)
    v_ref[0] = _dot(hb, w_ref[:, 2 * d:3 * d]).astype(BF16)


def _proj1(x, g_pre, shift, scale, w_in):
    nb, s, d = x.shape
    tm = TOKEN_TILE
    n = w_in.shape[1] // 3
    tok = lambda b, i: (b, i, 0)
    mod = pl.BlockSpec((1, 1, d), lambda b, i: (b, 0, 0))
    out = jax.ShapeDtypeStruct((nb, s, n), BF16)
    return pl.pallas_call(
        _proj1_kernel,
        grid=(nb, s // tm),
        in_specs=[pl.BlockSpec((1, tm, d), tok),
                  pl.BlockSpec((1, d), lambda b, i: (0, 0)), mod, mod,
                  _resident(w_in.shape, lambda b, i: (0, 0))],
        out_specs=(pl.BlockSpec((1, tm, n), tok),) * 3,
        out_shape=(out, out, out),
        compiler_params=_cparams(2),
        name="proj1",
    )(x, g_pre, shift, scale, w_in)


def _attn_c_kernel(q_ref, k_ref, v_ref, bias_ref, o_ref):
    rows = k_ref.shape[1] // GRID_W
    pair_tokens = 2 * GRID_W
    win = C_WIN_ROWS * GRID_W
    blk = pl.program_id(2)
    left = _left_lanes((pair_tokens, LANES))
    starts, variants = [], []
    for j in range(C_PAIRS):
        nominal = 2 * (blk * C_PAIRS + j) - NA_ROWS // 2
        ws = jnp.clip(nominal, 0, rows - C_WIN_ROWS)
        variants.append(2 - lax.shift_right_arithmetic(ws - nominal, 1))
        starts.append(pl.multiple_of(ws * GRID_W, GRID_W))
    toks = lambda j: slice(j * pair_tokens, (j + 1) * pair_tokens)

    def score(task):
        j, hd = task
        return _head_scores(q_ref[0, toks(j), :], k_ref[0, pl.ds(starts[j], win), :], hd) + bias_ref[hd, variants[j]]

    first = {}

    def finish(task, s):
        j, hd = task
        o, _ = _softmax_pv(s, v_ref[0, pl.ds(starts[j], win), :])
        if hd == 0:
            first[j] = o
        else:
            o_ref[0, toks(j), :] = jnp.where(left, first.pop(j), o).astype(BF16)

    _run_tasks([(j, hd) for j in range(C_PAIRS) for hd in range(2)], C_LOOKAHEAD, score, finish)


def _na_bias_table(rpb, rows):
    n_heads, n_dr, n_dc = rpb.shape
    n_pairs = rows // 2
    c = np.arange(GRID_W)
    cs = np.clip(c - NA_COLS // 2, 0, GRID_W - NA_COLS)
    kc = np.arange(GRID_W)
    valid_c = (kc[None, :] >= cs[:, None]) & (kc[None, :] < cs[:, None] + NA_COLS)
    span = 2 * GRID_W - 1
    side = (span - n_dc) // 2
    g = jnp.pad(rpb.astype(F32), ((0, 0), (0, 0), (side, side)), constant_values=NEG_BIG)
    flat = jnp.tile(g, (1, 1, GRID_W))
    toep = flat[:, :, GRID_W - 1:GRID_W - 1 + GRID_W * (span - 1)]
    toep = toep.reshape(n_heads, n_dr, GRID_W, span - 1)[..., :GRID_W]
    toep = jnp.where(valid_c[None, None], toep, NEG_BIG)
    masked = jnp.full((n_heads, GRID_W, GRID_W), NEG_BIG, F32)
    tiles = []
    for i in (0, 1, 2, n_pairs - 2, n_pairs - 1):
        ws = int(np.clip(2 * i - NA_ROWS // 2, 0, rows - C_WIN_ROWS))
        halves = []
        for qr in range(2):
            r = 2 * i + qr
            rs = int(np.clip(r - NA_ROWS // 2, 0, rows - NA_ROWS))
            blocks = []
            for kr in range(ws, ws + C_WIN_ROWS):
                blocks.append(toep[:, kr - r + NA_ROWS - 1] if rs <= kr < rs + NA_ROWS else masked)
            halves.append(jnp.concatenate(blocks, axis=-1))
        tiles.append(jnp.concatenate(halves, axis=1))
    return jnp.stack(tiles, axis=1)


def _attn_c(q, k, v, bias):
    nb, s, w = q.shape
    pairs = w // LANES
    step_tokens = C_PAIRS * 2 * GRID_W
    n_var, tq, win = bias.shape[1:]
    col = lambda b, hp, i: (b, 0, hp)
    return pl.pallas_call(
        _attn_c_kernel,
        grid=(nb, pairs, s // step_tokens),
        in_specs=[pl.BlockSpec((1, step_tokens, LANES), lambda b, hp, i: (b, i, hp)),
                  pl.BlockSpec((1, s, LANES), col),
                  pl.BlockSpec((1, s, LANES), col),
                  _resident((2, n_var, tq, win), lambda b, hp, i: (hp, 0, 0, 0))],
        out_specs=pl.BlockSpec((1, step_tokens, LANES), lambda b, hp, i: (b, i, hp)),
        out_shape=jax.ShapeDtypeStruct((nb, s, w), BF16),
        compiler_params=_cparams(3),
        name="attn_c",
    )(q, k, v, bias)


def _rope_parts(s, tm):
    def cos_sin(pos, dim):
        inv_freq = ROPE_THETA ** (-jnp.arange(0, dim, 2, dtype=F32) / dim)
        ang = pos.astype(F32)[:, None] * inv_freq[None, :]
        return jnp.cos(ang), jnp.sin(ang)

    rows = s // GRID_W
    cr, sr = cos_sin(jnp.arange(rows, dtype=jnp.int32), HEAD_DIM // 2)
    cc, sc = cos_sin(jnp.arange(GRID_W, dtype=jnp.int32), HEAD_DIM // 2)
    zr, zc = jnp.zeros_like(cr), jnp.zeros_like(cc)
    axial = (jnp.concatenate([cr, cr, zr, zr] * 2, axis=-1), jnp.concatenate([-sr, sr, zr, zr] * 2, axis=-1),
             jnp.concatenate([zc, zc, cc, cc] * 2, axis=-1), jnp.concatenate([zc, zc, -sc, sc] * 2, axis=-1))
    c0, s0 = cos_sin(jnp.arange(0, s, tm, dtype=jnp.int32), HEAD_DIM)
    cj, sj = cos_sin(jnp.arange(tm, dtype=jnp.int32), HEAD_DIM)
    wide = lambda a: jnp.concatenate([a] * 4, axis=-1)
    linear = (wide(c0)[:, None, :], wide(s0)[:, None, :], wide(cj), wide(sj))
    return axial + linear


def _head_sum_matrix(n):
    idx = np.arange(n) // HEAD_DIM
    return jnp.asarray(idx[:, None] == idx[None, :], BF16)


def kernel(x, c, ada_w, ada_b, norm_g, ab_w_in, ab_w_out, a_q_gain, a_k_gain,
           c_w_in, c_w_out, c_rpb, mlp_w_up, mlp_w_down):
    nb, s, d = x.shape
    assert s % (C_PAIRS * 2 * GRID_W) == 0 and s // GRID_W >= 2 * C_WIN_ROWS
    mods = _adaln(c, ada_w, ada_b).reshape(ada_w.shape[0], 2, nb, 3, d)

    def mod(layer, which):
        m = mods[layer, which]
        return m[:, 0:1], m[:, 1:2], m[:, 2:3]

    ropes = _rope_parts(s, TOKEN_TILE)
    gain_qk = jnp.concatenate([jnp.tile(a_q_gain[0] * (QK_SCALE * LOG2_E), A_Q_HEADS),
                               jnp.tile(a_k_gain[0], A_KV_HEADS)])[None, :]
    gmat = _head_sum_matrix(A_Q_W + A_KV_W)

    shift, scale, gate = mod(0, 0)
    qa, ka, vt, groups = _proj0(x, norm_g[0, 0:1], shift, scale, ab_w_in[0].astype(BF16), ropes, gain_qk, gmat)
    oa = _attn_a(qa, ka, vt)
    ob = _attn_b(groups)
    w_out = ab_w_out[0].astype(BF16)
    shift, scale, gate_mlp = mod(0, 1)
    x = _mix_mlp([oa, ob], [w_out[:A_Q_W], w_out[A_Q_W:]], x, norm_g[0, 1:2], gate,
                 norm_g[0, 2:3], shift, scale, mlp_w_up[0].astype(BF16), mlp_w_down[0].astype(BF16),
                 norm_g[0, 3:4], gate_mlp)

    shift, scale, gate = mod(1, 0)
    q, k, v = _proj1(x, norm_g[1, 0:1], shift, scale, c_w_in[0].astype(BF16))
    oc = _attn_c(q, k, v, _na_bias_table(c_rpb[0], s // GRID_W))
    shift, scale, gate_mlp = mod(1, 1)
    return _mix_mlp([oc], [c_w_out[0].astype(BF16)], x, norm_g[1, 1:2], gate,
                    norm_g[1, 2:3], shift, scale, mlp_w_up[1].astype(BF16), mlp_w_down[1].astype(BF16),
                    norm_g[1, 3:4], gate_mlp)
```

```python
import functools
import math

import numpy as np
import jax
import jax.numpy as jnp
from jax import lax
from jax.experimental import pallas as pl
from jax.experimental.pallas import tpu as pltpu

F32 = jnp.float32
BF16 = jnp.bfloat16

HEAD_DIM = 64
GRID_W = 64
ROPE_THETA = 10000.0
RMS_EPS = 1e-6
A_Q_HEADS = 8
A_KV_HEADS = 2
B_GROUPS = ((128, 1), (512, 4), (2048, 16))
B_HEADS_PER_GROUP = 4
C_HEADS = 16
NA_ROWS = 8
NA_COLS = 16
A_Q_W = A_Q_HEADS * HEAD_DIM
A_KV_W = A_KV_HEADS * HEAD_DIM
B_GROUP_W = B_HEADS_PER_GROUP * HEAD_DIM
B_W = len(B_GROUPS) * B_GROUP_W
AB_A_W = A_Q_W + 2 * A_KV_W
QK_SCALE = 1.0 / math.sqrt(HEAD_DIM)
LOG2_E = math.log2(math.e)

LANES = 128
V7X_VMEM_LIMIT_BYTES = 56 * 1024 * 1024
NEG_BIG = -1e30

TOKEN_TILE = 512
A_TQ = 256
A_TK = 256
A_K_W = A_KV_W
A_UNROLL = 16
A_SAFE_UNROLL = 2
A_MIN_DENOM = 2.0 ** -60
A_LOOKAHEAD = 8
A_SCORE_RING = 16
A_V_ROWS = 80
B_TILE = 2048
B_TQ = 128
B_HALF = 64
B_SUBS_PER_ITER = 4
B_LOOKAHEAD = 4
C_LOOKAHEAD = 3
C_PAIRS = 8
C_WIN_ROWS = 10


def _cparams(n_axes):
    return pltpu.CompilerParams(dimension_semantics=("arbitrary",) * n_axes,
                                vmem_limit_bytes=V7X_VMEM_LIMIT_BYTES)


def _resident(block_shape, index_map):
    return pl.BlockSpec(block_shape, index_map, pipeline_mode=pl.Buffered(1))


def _rms(x):
    return x * lax.rsqrt(jnp.mean(x * x, axis=-1, keepdims=True) + RMS_EPS)


def _dot(a, b):
    return jnp.dot(a, b, preferred_element_type=F32)


def _dot_nt(a, b):
    return lax.dot_general(a, b, (((1,), (1,)), ((), ())), preferred_element_type=F32)


def _rope_slab(x, cos, sin_signed, half):
    lane = lax.broadcasted_iota(jnp.int32, x.shape, 1)
    first = (lane & half) == 0
    rot = jnp.where(first, pltpu.roll(x, LANES - half, 1), pltpu.roll(x, half, 1))
    return x * cos + rot * sin_signed


def _adaln_kernel(ct_ref, w_ref, b_ref, o_ref):
    ct = ct_ref[...]
    cond = ct * jax.nn.sigmoid(ct)
    w = w_ref[0]
    rows = [jnp.sum(w * cond[:, b:b + 1], axis=0, keepdims=True) for b in range(ct.shape[1])]
    o_ref[0] = jnp.concatenate(rows, axis=0) + b_ref[0]


def _adaln(c, ada_w, ada_b):
    nb, d = c.shape
    n_mod = ada_w.shape[0] * ada_w.shape[1]
    n_out = ada_w.shape[-1]
    tn = 768
    w = ada_w.reshape(n_mod, d, n_out)
    b = ada_b.reshape(n_mod, 1, n_out)
    return pl.pallas_call(
        _adaln_kernel,
        grid=(n_mod, n_out // tn),
        in_specs=[pl.BlockSpec((d, nb), lambda m, j: (0, 0)),
                  pl.BlockSpec((1, d, tn), lambda m, j: (m, 0, j)),
                  pl.BlockSpec((1, 1, tn), lambda m, j: (m, 0, j))],
        out_specs=pl.BlockSpec((1, nb, tn), lambda m, j: (m, 0, j)),
        out_shape=jax.ShapeDtypeStruct((n_mod, nb, n_out), F32),
        compiler_params=_cparams(2),
        name="adaln",
    )(c.T, w, b)


def _proj0_kernel(x_ref, g_ref, sh_ref, sc_ref, w_ref, rc_ref, rs_ref, cc_ref, cs_ref, c0_ref, s0_ref, cj_ref, sj_ref,
                  gain_ref, gmat_ref, qa_ref, ka_ref, vt_ref, *rest):
    b_refs, stage_ref = rest[:-1], rest[-1]
    tm = x_ref.shape[1]
    col_c, col_s = cc_ref[...], cs_ref[...]
    ca = jnp.concatenate([rc_ref[r:r + 1, :] + col_c for r in range(tm // GRID_W)], axis=0)
    sa = jnp.concatenate([rs_ref[r:r + 1, :] + col_s for r in range(tm // GRID_W)], axis=0)
    c0, s0, cj, sj = c0_ref[0], s0_ref[0], cj_ref[...], sj_ref[...]
    lane = lax.broadcasted_iota(jnp.int32, (tm, LANES), 1)
    cb = c0 * cj - s0 * sj
    sb = jnp.where((lane & (HEAD_DIM // 2)) == 0, -1.0, 1.0) * (s0 * cj + c0 * sj)
    x = x_ref[0]
    h = _rms(x) * g_ref[...] * (1.0 + sc_ref[0]) + sh_ref[0]
    hb = h.astype(BF16)

    pa = _dot(hb, w_ref[:, 0:AB_A_W])
    n_qk = A_Q_W + A_KV_W
    qk = pa[:, 0:n_qk]
    sq = qk * qk
    hi = sq.astype(BF16)
    lo = (sq - hi.astype(F32)).astype(BF16)
    ms = (_dot(hi, gmat_ref[...]) + _dot(lo, gmat_ref[...])) * (1.0 / HEAD_DIM)
    qk = qk * lax.rsqrt(ms + RMS_EPS) * gain_ref[...]
    for j in range(n_qk // LANES):
        slab = _rope_slab(qk[:, j * LANES:(j + 1) * LANES], ca, sa, HEAD_DIM // 4).astype(BF16)
        if j < A_Q_W // LANES:
            qa_ref[0, :, j * LANES:(j + 1) * LANES] = slab
        else:
            ka_ref[0] = slab
    vt = pa[:, n_qk:AB_A_W].T
    ones = jnp.ones((A_V_ROWS - HEAD_DIM, A_TK), BF16)
    for kvh in range(A_KV_HEADS):
        for c in range(tm // A_TK):
            vt_ref[0, kvh, c, 0:HEAD_DIM, :] = vt[kvh * HEAD_DIM:(kvh + 1) * HEAD_DIM,
                                                  c * A_TK:(c + 1) * A_TK].astype(BF16)
            vt_ref[0, kvh, c, HEAD_DIM:A_V_ROWS, :] = ones

    pb = _dot(hb, w_ref[:, AB_A_W:AB_A_W + 3 * B_W])
    slot = 0
    for kind in range(3):
        for g, (_, dil) in enumerate(B_GROUPS):
            out_ref = b_refs[3 * g + kind]
            for hp in range(B_GROUP_W // LANES):
                lanes = slice(hp * LANES, (hp + 1) * LANES)
                col = kind * B_W + g * B_GROUP_W + hp * LANES
                slab = pb[:, col:col + LANES]
                if kind == 0:
                    slab = _rope_slab(slab, cb, sb, HEAD_DIM // 2) * (QK_SCALE * LOG2_E)
                elif kind == 1:
                    slab = _rope_slab(slab, cb, sb, HEAD_DIM // 2)
                if dil == 1:
                    out_ref[0, 0, :, lanes] = slab.astype(BF16)
                    continue
                stage_ref[slot] = slab
                for rho in range(dil):
                    out_ref[0, rho, :, lanes] = stage_ref[slot, pl.ds(rho, tm // dil, stride=dil), :].astype(BF16)
                slot += 1


def _proj0(x, g_pre, shift, scale, w_in, ropes, gain_qk, gmat):
    nb, s, d = x.shape
    tm = TOKEN_TILE
    tok = lambda b, i: (b, i, 0)
    mod = lambda b, i: (b, 0, 0)
    row_spec = pl.BlockSpec((tm // GRID_W, LANES), lambda b, i: (i, 0))
    col_spec = pl.BlockSpec((GRID_W, LANES), lambda b, i: (0, 0))
    base_spec = pl.BlockSpec((1, 1, LANES), lambda b, i: (i, 0, 0))
    in_tile_spec = pl.BlockSpec((tm, LANES), lambda b, i: (0, 0))
    n_qk = A_Q_W + A_KV_W
    out_shape = [
        jax.ShapeDtypeStruct((nb, s, A_Q_W), BF16),
        jax.ShapeDtypeStruct((nb, s, A_K_W), BF16),
        jax.ShapeDtypeStruct((nb, A_KV_HEADS, s // A_TK, A_V_ROWS, A_TK), BF16),
    ]
    out_specs = [
        pl.BlockSpec((1, tm, A_Q_W), tok),
        pl.BlockSpec((1, tm, A_K_W), tok),
        pl.BlockSpec((1, A_KV_HEADS, tm // A_TK, A_V_ROWS, A_TK), lambda b, i: (b, 0, i, 0, 0)),
    ]
    n_staged = 0
    for _, dil in B_GROUPS:
        for _ in range(3):
            out_shape.append(jax.ShapeDtypeStruct((nb, dil, s // dil, B_GROUP_W), BF16))
            out_specs.append(pl.BlockSpec((1, dil, tm // dil, B_GROUP_W), lambda b, i: (b, 0, i, 0)))
        n_staged += 3 * (B_GROUP_W // LANES) * (dil > 1)
    outs = pl.pallas_call(
        _proj0_kernel,
        grid=(nb, s // tm),
        in_specs=[pl.BlockSpec((1, tm, d), tok),
                  pl.BlockSpec((1, d), lambda b, i: (0, 0)),
                  pl.BlockSpec((1, 1, d), mod),
                  pl.BlockSpec((1, 1, d), mod),
                  _resident(w_in.shape, lambda b, i: (0, 0)),
                  row_spec, row_spec, col_spec, col_spec, base_spec, base_spec, in_tile_spec, in_tile_spec,
                  pl.BlockSpec((1, n_qk), lambda b, i: (0, 0)),
                  _resident((n_qk, n_qk), lambda b, i: (0, 0))],
        out_specs=out_specs,
        out_shape=out_shape,
        scratch_shapes=[pltpu.VMEM((n_staged, tm, LANES), F32)],
        compiler_params=_cparams(2),
        name="proj0",
    )(x, g_pre, shift, scale, w_in, *ropes, gain_qk, gmat)
    qa, ka, vt = outs[:3]
    groups = [tuple(outs[3 + 3 * g:6 + 3 * g]) for g in range(len(B_GROUPS))]
    return qa, ka, vt, groups


def _attn_a_kernel(q_ref, k_ref, vt_ref, gsum_ref, o_ref, qbd_ref, s_ref, kmax_ref):
    tq = q_ref.shape[1]
    n_chunks = k_ref.shape[1] // A_TK
    group = A_Q_HEADS // A_KV_HEADS
    ring = s_ref.shape[0]

    @pl.when(pl.program_id(1) == 0)
    def _():
        def longest(c, best):
            kk = k_ref[0, pl.ds(pl.multiple_of(c * A_TK, A_TK), A_TK), 0:A_KV_W].astype(F32)
            norms = _dot((kk * kk).astype(BF16), gsum_ref[...])
            return jnp.maximum(best, jnp.max(norms, axis=0, keepdims=True))
        kmax_ref[...] = lax.fori_loop(0, n_chunks, longest, jnp.zeros((1, A_KV_W), F32))

    qt = q_ref[0].astype(F32).T
    zeros = jnp.zeros((HEAD_DIM, tq), F32)
    shifts = []
    for h in range(A_Q_HEADS):
        kvh = h // group
        blk = qt[h * HEAD_DIM:(h + 1) * HEAD_DIM]
        shifts.append(jnp.sqrt(jnp.sum(blk * blk, axis=0, keepdims=True)
                               * kmax_ref[:, kvh * HEAD_DIM:kvh * HEAD_DIM + 1]))
        full = jnp.concatenate([blk, zeros] if kvh == 0 else [zeros, blk], axis=0)
        qbd_ref[h] = full.astype(BF16)

    def issue_score(slot, chunk, h):
        k = k_ref[0, pl.ds(pl.multiple_of(chunk * A_TK, A_TK), A_TK), :]
        s_ref[slot] = _dot(k, qbd_ref[h])

    def attend(unroll, shifted):
        strips = [(u, h) for u in range(unroll) for h in range(A_Q_HEADS)]
        n_strips = len(strips)
        assert n_strips % ring == 0 and A_LOOKAHEAD < ring and n_chunks % unroll == 0

        def body(c, carry):
            ms, accs = list(carry[0]), list(carry[1])
            next_base = jnp.minimum(c + 1, n_chunks // unroll - 1) * unroll
            for idx, (u, h) in enumerate(strips):
                nxt = idx + A_LOOKAHEAD
                if nxt < n_strips:
                    issue_score(nxt % ring, c * unroll + strips[nxt][0], strips[nxt][1])
                else:
                    u2, h2 = strips[nxt - n_strips]
                    issue_score(nxt % ring, next_base + u2, h2)
                slot = idx % ring
                vt = vt_ref[0, h // group, c * unroll + u]
                if shifted:
                    accs[h] = accs[h] + _dot(vt, jnp.exp2(s_ref[slot] - shifts[h]).astype(BF16))
                    continue
                m_new = jnp.maximum(ms[h], jnp.max(s_ref[slot], axis=0, keepdims=True))
                alpha = jnp.exp2(ms[h] - m_new)
                p = jnp.exp2(s_ref[slot] - m_new).astype(BF16)
                ms[h] = m_new
                accs[h] = accs[h] * alpha + _dot(vt, p)
            return tuple(ms), tuple(accs)

        for idx in range(A_LOOKAHEAD):
            issue_score(idx, strips[idx][0], strips[idx][1])
        init = (tuple(jnp.full((1, tq), -jnp.inf, F32) for _ in range(0 if shifted else A_Q_HEADS)),
                tuple(jnp.zeros((A_V_ROWS, tq), F32) for _ in range(A_Q_HEADS)))
        return lax.fori_loop(0, n_chunks // unroll, body, init)[1]

    def finish(accs):
        outs = [a[0:HEAD_DIM] / a[HEAD_DIM:HEAD_DIM + 1] for a in accs]
        o_ref[0] = jnp.concatenate(outs, axis=0).T.astype(BF16)

    accs = attend(A_UNROLL, True)
    finish(accs)
    smallest = jnp.min(functools.reduce(jnp.minimum, [a[HEAD_DIM:HEAD_DIM + 1] for a in accs]))

    @pl.when(jnp.logical_not(smallest > A_MIN_DENOM))
    def _():
        finish(attend(A_SAFE_UNROLL, False))


def _attn_a(qa, ka, vt):
    nb, s, _ = qa.shape
    tq = A_TQ
    gsum = _head_sum_matrix(A_KV_W)
    return pl.pallas_call(
        _attn_a_kernel,
        grid=(nb, s // tq),
        in_specs=[pl.BlockSpec((1, tq, A_Q_W), lambda b, i: (b, i, 0)),
                  _resident((1, s, A_K_W), lambda b, i: (b, 0, 0)),
                  _resident((1,) + vt.shape[1:], lambda b, i: (b, 0, 0, 0, 0)),
                  _resident(gsum.shape, lambda b, i: (0, 0))],
        out_specs=pl.BlockSpec((1, tq, A_Q_W), lambda b, i: (b, i, 0)),
        out_shape=jax.ShapeDtypeStruct((nb, s, A_Q_W), BF16),
        scratch_shapes=[pltpu.VMEM((A_Q_HEADS, A_K_W, tq), BF16),
                        pltpu.VMEM((A_SCORE_RING, A_TK, tq), F32),
                        pltpu.VMEM((1, A_KV_W), F32)],
        compiler_params=_cparams(2),
        name="attn_a",
    )(qa, ka, vt, gsum)


def _left_lanes(shape):
    return lax.broadcasted_iota(jnp.int32, shape, 1) < HEAD_DIM


def _head_scores(q, kw, hd):
    left = _left_lanes(q.shape)
    zero = jnp.zeros_like(q)
    qh = jnp.where(left, q, zero) if hd == 0 else jnp.where(left, zero, q)
    return _dot_nt(qh, kw)


def _softmax_pv(s, vw):
    m = jnp.max(s, axis=-1, keepdims=True)
    p = jnp.exp(s - m)
    l = jnp.sum(p, axis=-1, keepdims=True)
    o = _dot(p.astype(BF16), vw) * (1.0 / l)
    return o, m + jnp.log(l)


def _run_tasks(tasks, lookahead, score, finish):
    pending = [score(t) for t in tasks[:lookahead]]
    for idx, t in enumerate(tasks):
        if idx + lookahead < len(tasks):
            pending.append(score(tasks[idx + lookahead]))
        finish(t, pending.pop(0))


def _attn_b_kernel(*refs):
    n_g = len(B_GROUPS)
    q_refs, k_refs, v_refs = refs[0:n_g], refs[n_g:2 * n_g], refs[2 * n_g:3 * n_g]
    bias_ref, o_ref = refs[3 * n_g], refs[3 * n_g + 1]
    o_scr, l_scr = refs[3 * n_g + 2:4 * n_g + 2], refs[4 * n_g + 2:5 * n_g + 2]
    tile_i = pl.program_id(2)
    left = _left_lanes((B_TQ, LANES))
    n_sub = B_TILE // B_TQ
    win = B_TQ + 2 * B_HALF

    def body(it, carry):
        geo = {}
        for j in range(B_SUBS_PER_ITER):
            k = it * B_SUBS_PER_ITER + j
            for g, (_, dil) in enumerate(B_GROUPS):
                per_res = n_sub // dil
                shift = per_res.bit_length() - 1
                rho = lax.shift_right_logical(k, jnp.int32(shift))
                sub = k & (per_res - 1)
                u0 = tile_i * (B_TILE // dil) + sub * B_TQ
                ws = jnp.clip(u0 - B_HALF, 0, k_refs[g].shape[2] - win)
                var = 1 - lax.shift_right_arithmetic(ws - (u0 - B_HALF), jnp.int32(B_HALF.bit_length() - 1))
                geo[(j, g)] = (rho, sub, pl.multiple_of(ws, B_HALF), var, dil)

        def score(task):
            j, g, hd = task
            rho, sub, ws, var, _ = geo[(j, g)]
            q = q_refs[g][0, rho, pl.ds(pl.multiple_of(sub * B_TQ, B_TQ), B_TQ), :]
            return _head_scores(q, k_refs[g][0, rho, pl.ds(ws, win), :], hd) + bias_ref[var]

        first = {}

        def finish(task, s):
            j, g, hd = task
            rho, sub, ws, _, dil = geo[(j, g)]
            m = jnp.max(s, axis=-1, keepdims=True)
            p = jnp.exp2(s - m)
            l = jnp.sum(p, axis=-1, keepdims=True)
            o = _dot(p.astype(BF16), v_refs[g][0, rho, pl.ds(ws, win), :]) * (1.0 / l)
            lse = m + jnp.log2(l)
            if hd == 0:
                first[(j, g)] = (o, lse)
                return
            o0, lse0 = first.pop((j, g))
            row0 = rho + (dil * B_TQ) * sub
            rows = pl.ds(row0, B_TQ, stride=dil) if dil > 1 else pl.ds(pl.multiple_of(row0, B_TQ), B_TQ)
            o_scr[g][rows, :] = jnp.where(left, o0, o)
            l_scr[g][rows, :] = jnp.where(left, lse0, lse)

        tasks = [(j, g, hd) for j in range(B_SUBS_PER_ITER) for g in range(n_g) for hd in range(2)]
        _run_tasks(tasks, B_LOOKAHEAD, score, finish)
        return carry

    lax.fori_loop(0, n_sub // B_SUBS_PER_ITER, body, 0)
    lses = [r[...] for r in l_scr]
    m = functools.reduce(jnp.maximum, lses)
    es = [jnp.exp2(l - m) for l in lses]
    num = functools.reduce(lambda a, b: a + b, [e * r[...] for e, r in zip(es, o_scr)])
    o_ref[0] = (num / functools.reduce(lambda a, b: a + b, es)).astype(BF16)


def _attn_b(groups):
    nb, _, s, _ = groups[0][0].shape
    assert s % B_TILE == 0 and all(s // dil >= B_TQ + 2 * B_HALF for _, dil in B_GROUPS)
    assert all(window // (2 * dil) == B_HALF and (B_TILE // B_TQ) % dil == 0 for window, dil in B_GROUPS)
    win = B_TQ + 2 * B_HALF
    shift = jnp.asarray([B_HALF, 0, -B_HALF], jnp.int32)[:, None, None]
    du = (jnp.arange(win, dtype=jnp.int32)[None, None, :] - jnp.arange(B_TQ, dtype=jnp.int32)[None, :, None]
          + shift - B_HALF)
    bias = jnp.where(jnp.abs(du) <= B_HALF, 0.0, NEG_BIG).astype(F32)
    q_specs, kv_specs = [], []
    for _, dil in B_GROUPS:
        q_specs.append(pl.BlockSpec((1, dil, B_TILE // dil, LANES), lambda b, hp, i: (b, 0, i, hp)))
        kv_specs.append(_resident((1, dil, s // dil, LANES), lambda b, hp, i: (b, 0, 0, hp)))
    qs, ks, vs = zip(*groups)
    return pl.pallas_call(
        _attn_b_kernel,
        grid=(nb, B_GROUP_W // LANES, s // B_TILE),
        in_specs=q_specs + kv_specs + kv_specs + [_resident(bias.shape, lambda b, hp, i: (0, 0, 0))],
        out_specs=pl.BlockSpec((1, B_TILE, LANES), lambda b, hp, i: (b, i, hp)),
        out_shape=jax.ShapeDtypeStruct((nb, s, B_GROUP_W), BF16),
        scratch_shapes=[pltpu.VMEM((B_TILE, LANES), F32) for _ in range(2 * len(B_GROUPS))],
        compiler_params=_cparams(3),
        name="attn_b",
    )(*qs, *ks, *vs, bias)


def _finish(y, x, g_post, gate):
    return x + gate * (_rms(y) * g_post)


def _mix_mlp_kernel(*refs, n_parts):
    parts, weights = refs[:n_parts], refs[n_parts:2 * n_parts]
    (x_ref, gmix_ref, gate_mix_ref, gpre_ref, sh_ref, sc_ref, wu_ref, wd_ref, gpost_ref, gate_ref,
     y_ref) = refs[2 * n_parts:]
    y = _dot(parts[0][0], weights[0][...])
    for o_ref, w_ref in zip(parts[1:], weights[1:]):
        y = y + _dot(o_ref[0], w_ref[...])
    x = _finish(y, x_ref[0], gmix_ref[...], gate_mix_ref[0])
    hb = (_rms(x) * gpre_ref[...] * (1.0 + sc_ref[0]) + sh_ref[0]).astype(BF16)
    d = x.shape[-1]
    acc = jnp.zeros(x.shape, F32)
    for j in range(wu_ref.shape[1] // d):
        u = jnp.maximum(_dot(hb, wu_ref[:, j * d:(j + 1) * d]), 0.0)
        acc = acc + _dot((u * u).astype(BF16), wd_ref[j * d:(j + 1) * d, :])
    y_ref[0] = _finish(acc, x, gpost_ref[...], gate_ref[0])


def _mix_mlp(parts, w_outs, x, g_mix, gate_mix, g_pre, shift, scale, w_up, w_down, g_post, gate):
    nb, s, d = x.shape
    tm = TOKEN_TILE
    tok = lambda b, i: (b, i, 0)
    vec = pl.BlockSpec((1, d), lambda b, i: (0, 0))
    mod = pl.BlockSpec((1, 1, d), lambda b, i: (b, 0, 0))
    const = lambda a: _resident(a.shape, lambda b, i: (0, 0))
    return pl.pallas_call(
        functools.partial(_mix_mlp_kernel, n_parts=len(parts)),
        grid=(nb, s // tm),
        in_specs=[pl.BlockSpec((1, tm, p.shape[-1]), tok) for p in parts] + [const(w) for w in w_outs] +
                 [pl.BlockSpec((1, tm, d), tok), vec, mod, vec, mod, mod, const(w_up), const(w_down), vec, mod],
        out_specs=pl.BlockSpec((1, tm, d), tok),
        out_shape=jax.ShapeDtypeStruct(x.shape, F32),
        compiler_params=_cparams(2),
        name="mix_mlp",
    )(*parts, *w_outs, x, g_mix, gate_mix, g_pre, shift, scale, w_up, w_down, g_post, gate)


def _proj1_kernel(x_ref, g_ref, sh_ref, sc_ref, w_ref, q_ref, k_ref, v_ref):
    hb = (_rms(x_ref[0]) * g_ref[...] * (1.0 + sc_ref[0]) + sh_ref[0]).astype(BF16)
    d = q_ref.shape[-1]
    q_ref[0] = (_dot(hb, w_ref[:, 0:d]) * QK_SCALE).astype(BF16)
    k_ref[0] = _dot(hb, w_ref[:, d:2 * d]).astype(BF16)
    v_ref[0] = _dot(hb, w_ref[:, 2 * d:3 * d]).astype(BF16)


def _proj1(x, g_pre, shift, scale, w_in):
    nb, s, d = x.shape
    tm = TOKEN_TILE
    n = w_in.shape[1] // 3
    tok = lambda b, i: (b, i, 0)
    mod = pl.BlockSpec((1, 1, d), lambda b, i: (b, 0, 0))
    out = jax.ShapeDtypeStruct((nb, s, n), BF16)
    return pl.pallas_call(
        _proj1_kernel,
        grid=(nb, s // tm),
        in_specs=[pl.BlockSpec((1, tm, d), tok),
                  pl.BlockSpec((1, d), lambda b, i: (0, 0)), mod, mod,
                  _resident(w_in.shape, lambda b, i: (0, 0))],
        out_specs=(pl.BlockSpec((1, tm, n), tok),) * 3,
        out_shape=(out, out, out),
        compiler_params=_cparams(2),
        name="proj1",
    )(x, g_pre, shift, scale, w_in)


def _attn_c_kernel(q_ref, k_ref, v_ref, bias_ref, o_ref):
    rows = k_ref.shape[1] // GRID_W
    pair_tokens = 2 * GRID_W
    win = C_WIN_ROWS * GRID_W
    blk = pl.program_id(2)
    left = _left_lanes((pair_tokens, LANES))
    starts, variants = [], []
    for j in range(C_PAIRS):
        nominal = 2 * (blk * C_PAIRS + j) - NA_ROWS // 2
        ws = jnp.clip(nominal, 0, rows - C_WIN_ROWS)
        variants.append(2 - lax.shift_right_arithmetic(ws - nominal, 1))
        starts.append(pl.multiple_of(ws * GRID_W, GRID_W))
    toks = lambda j: slice(j * pair_tokens, (j + 1) * pair_tokens)

    def score(task):
        j, hd = task
        return _head_scores(q_ref[0, toks(j), :], k_ref[0, pl.ds(starts[j], win), :], hd) + bias_ref[hd, variants[j]]

    first = {}

    def finish(task, s):
        j, hd = task
        o, _ = _softmax_pv(s, v_ref[0, pl.ds(starts[j], win), :])
        if hd == 0:
            first[j] = o
        else:
            o_ref[0, toks(j), :] = jnp.where(left, first.pop(j), o).astype(BF16)

    _run_tasks([(j, hd) for j in range(C_PAIRS) for hd in range(2)], C_LOOKAHEAD, score, finish)


def _na_bias_table(rpb, rows):
    n_heads, n_dr, n_dc = rpb.shape
    n_pairs = rows // 2
    c = np.arange(GRID_W)
    cs = np.clip(c - NA_COLS // 2, 0, GRID_W - NA_COLS)
    kc = np.arange(GRID_W)
    valid_c = (kc[None, :] >= cs[:, None]) & (kc[None, :] < cs[:, None] + NA_COLS)
    span = 2 * GRID_W - 1
    side = (span - n_dc) // 2
    g = jnp.pad(rpb.astype(F32), ((0, 0), (0, 0), (side, side)), constant_values=NEG_BIG)
    flat = jnp.tile(g, (1, 1, GRID_W))
    toep = flat[:, :, GRID_W - 1:GRID_W - 1 + GRID_W * (span - 1)]
    toep = toep.reshape(n_heads, n_dr, GRID_W, span - 1)[..., :GRID_W]
    toep = jnp.where(valid_c[None, None], toep, NEG_BIG)
    masked = jnp.full((n_heads, GRID_W, GRID_W), NEG_BIG, F32)
    tiles = []
    for i in (0, 1, 2, n_pairs - 2, n_pairs - 1):
        ws = int(np.clip(2 * i - NA_ROWS // 2, 0, rows - C_WIN_ROWS))
        halves = []
        for qr in range(2):
            r = 2 * i + qr
            rs = int(np.clip(r - NA_ROWS // 2, 0, rows - NA_ROWS))
            blocks = []
            for kr in range(ws, ws + C_WIN_ROWS):
                blocks.append(toep[:, kr - r + NA_ROWS - 1] if rs <= kr < rs + NA_ROWS else masked)
            halves.append(jnp.concatenate(blocks, axis=-1))
        tiles.append(jnp.concatenate(halves, axis=1))
    return jnp.stack(tiles, axis=1)


def _attn_c(q, k, v, bias):
    nb, s, w = q.shape
    pairs = w // LANES
    step_tokens = C_PAIRS * 2 * GRID_W
    n_var, tq, win = bias.shape[1:]
    col = lambda b, hp, i: (b, 0, hp)
    return pl.pallas_call(
        _attn_c_kernel,
        grid=(nb, pairs, s // step_tokens),
        in_specs=[pl.BlockSpec((1, step_tokens, LANES), lambda b, hp, i: (b, i, hp)),
                  pl.BlockSpec((1, s, LANES), col),
                  pl.BlockSpec((1, s, LANES), col),
                  _resident((2, n_var, tq, win), lambda b, hp, i: (hp, 0, 0, 0))],
        out_specs=pl.BlockSpec((1, step_tokens, LANES), lambda b, hp, i: (b, i, hp)),
        out_shape=jax.ShapeDtypeStruct((nb, s, w), BF16),
        compiler_params=_cparams(3),
        name="attn_c",
    )(q, k, v, bias)


def _rope_parts(s, tm):
    def cos_sin(pos, dim):
        inv_freq = ROPE_THETA ** (-jnp.arange(0, dim, 2, dtype=F32) / dim)
        ang = pos.astype(F32)[:, None] * inv_freq[None, :]
        return jnp.cos(ang), jnp.sin(ang)

    rows = s // GRID_W
    cr, sr = cos_sin(jnp.arange(rows, dtype=jnp.int32), HEAD_DIM // 2)
    cc, sc = cos_sin(jnp.arange(GRID_W, dtype=jnp.int32), HEAD_DIM // 2)
    zr, zc = jnp.zeros_like(cr), jnp.zeros_like(cc)
    axial = (jnp.concatenate([cr, cr, zr, zr] * 2, axis=-1), jnp.concatenate([-sr, sr, zr, zr] * 2, axis=-1),
             jnp.concatenate([zc, zc, cc, cc] * 2, axis=-1), jnp.concatenate([zc, zc, -sc, sc] * 2, axis=-1))
    c0, s0 = cos_sin(jnp.arange(0, s, tm, dtype=jnp.int32), HEAD_DIM)
    cj, sj = cos_sin(jnp.arange(tm, dtype=jnp.int32), HEAD_DIM)
    wide = lambda a: jnp.concatenate([a] * 4, axis=-1)
    linear = (wide(c0)[:, None, :], wide(s0)[:, None, :], wide(cj), wide(sj))
    return axial + linear


def _head_sum_matrix(n):
    idx = np.arange(n) // HEAD_DIM
    return jnp.asarray(idx[:, None] == idx[None, :], BF16)


def kernel(x, c, ada_w, ada_b, norm_g, ab_w_in, ab_w_out, a_q_gain, a_k_gain,
           c_w_in, c_w_out, c_rpb, mlp_w_up, mlp_w_down):
    nb, s, d = x.shape
    assert s % (C_PAIRS * 2 * GRID_W) == 0 and s // GRID_W >= 2 * C_WIN_ROWS
    mods = _adaln(c, ada_w, ada_b).reshape(ada_w.shape[0], 2, nb, 3, d)

    def mod(layer, which):
        m = mods[layer, which]
        return m[:, 0:1], m[:, 1:2], m[:, 2:3]

    ropes = _rope_parts(s, TOKEN_TILE)
    gain_qk = jnp.concatenate([jnp.tile(a_q_gain[0] * (QK_SCALE * LOG2_E), A_Q_HEADS),
                               jnp.tile(a_k_gain[0], A_KV_HEADS)])[None, :]
    gmat = _head_sum_matrix(A_Q_W + A_KV_W)

    shift, scale, gate = mod(0, 0)
    qa, ka, vt, groups = _proj0(x, norm_g[0, 0:1], shift, scale, ab_w_in[0].astype(BF16), ropes, gain_qk, gmat)
    oa = _attn_a(qa, ka, vt)
    ob = _attn_b(groups)
    w_out = ab_w_out[0].astype(BF16)
    shift, scale, gate_mlp = mod(0, 1)
    x = _mix_mlp([oa, ob], [w_out[:A_Q_W], w_out[A_Q_W:]], x, norm_g[0, 1:2], gate,
                 norm_g[0, 2:3], shift, scale, mlp_w_up[0].astype(BF16), mlp_w_down[0].astype(BF16),
                 norm_g[0, 3:4], gate_mlp)

    shift, scale, gate = mod(1, 0)
    q, k, v = _proj1(x, norm_g[1, 0:1], shift, scale, c_w_in[0].astype(BF16))
    oc = _attn_c(q, k, v, _na_bias_table(c_rpb[0], s // GRID_W))
    shift, scale, gate_mlp = mod(1, 1)
    return _mix_mlp([oc], [c_w_out[0].astype(BF16)], x, norm_g[1, 1:2], gate,
                    norm_g[1, 2:3], shift, scale, mlp_w_up[1].astype(BF16), mlp_w_down[1].astype(BF16),
                    norm_g[1, 3:4], gate_mlp)
```

```python
import functools
import math

import numpy as np
import jax
import jax.numpy as jnp
from jax import lax
from jax.experimental import pallas as pl
from jax.experimental.pallas import tpu as pltpu

F32 = jnp.float32
BF16 = jnp.bfloat16

HEAD_DIM = 64
GRID_W = 64
ROPE_THETA = 10000.0
RMS_EPS = 1e-6
A_Q_HEADS = 8
A_KV_HEADS = 2
B_GROUPS = ((128, 1), (512, 4), (2048, 16))
B_HEADS_PER_GROUP = 4
C_HEADS = 16
NA_ROWS = 8
NA_COLS = 16
A_Q_W = A_Q_HEADS * HEAD_DIM
A_KV_W = A_KV_HEADS * HEAD_DIM
B_GROUP_W = B_HEADS_PER_GROUP * HEAD_DIM
B_W = len(B_GROUPS) * B_GROUP_W
AB_A_W = A_Q_W + 2 * A_KV_W
QK_SCALE = 1.0 / math.sqrt(HEAD_DIM)
LOG2_E = math.log2(math.e)

LANES = 128
V7X_VMEM_LIMIT_BYTES = 56 * 1024 * 1024
NEG_BIG = -1e30

TOKEN_TILE = 512
A_TQ = 256
A_TK = 256
A_K_W = A_KV_W
A_UNROLL = 16
A_SAFE_UNROLL = 2
A_MIN_DENOM = 2.0 ** -60
A_LOOKAHEAD = 4
A_SCORE_RING = 16
A_V_ROWS = 80
B_TILE = 2048
B_TQ = 128
B_HALF = 64
B_SUBS_PER_ITER = 4
B_LOOKAHEAD = 4
C_LOOKAHEAD = 3
C_PAIRS = 16
C_WIN_ROWS = 10


def _cparams(n_axes):
    return pltpu.CompilerParams(dimension_semantics=("arbitrary",) * n_axes,
                                vmem_limit_bytes=V7X_VMEM_LIMIT_BYTES)


def _resident(block_shape, index_map):
    return pl.BlockSpec(block_shape, index_map, pipeline_mode=pl.Buffered(1))


def _rms(x):
    return x * lax.rsqrt(jnp.mean(x * x, axis=-1, keepdims=True) + RMS_EPS)


def _dot(a, b):
    return jnp.dot(a, b, preferred_element_type=F32)


def _dot_nt(a, b):
    return lax.dot_general(a, b, (((1,), (1,)), ((), ())), preferred_element_type=F32)


def _rope_slab(x, cos, sin_signed, half):
    lane = lax.broadcasted_iota(jnp.int32, x.shape, 1)
    first = (lane & half) == 0
    rot = jnp.where(first, pltpu.roll(x, LANES - half, 1), pltpu.roll(x, half, 1))
    return x * cos + rot * sin_signed


def _adaln_kernel(ct_ref, w_ref, b_ref, o_ref):
    ct = ct_ref[...]
    cond = ct * jax.nn.sigmoid(ct)
    w = w_ref[0]
    rows = [jnp.sum(w * cond[:, b:b + 1], axis=0, keepdims=True) for b in range(ct.shape[1])]
    o_ref[0] = jnp.concatenate(rows, axis=0) + b_ref[0]


def _adaln(c, ada_w, ada_b):
    nb, d = c.shape
    n_mod = ada_w.shape[0] * ada_w.shape[1]
    n_out = ada_w.shape[-1]
    tn = 768
    w = ada_w.reshape(n_mod, d, n_out)
    b = ada_b.reshape(n_mod, 1, n_out)
    return pl.pallas_call(
        _adaln_kernel,
        grid=(n_mod, n_out // tn),
        in_specs=[pl.BlockSpec((d, nb), lambda m, j: (0, 0)),
                  pl.BlockSpec((1, d, tn), lambda m, j: (m, 0, j)),
                  pl.BlockSpec((1, 1, tn), lambda m, j: (m, 0, j))],
        out_specs=pl.BlockSpec((1, nb, tn), lambda m, j: (m, 0, j)),
        out_shape=jax.ShapeDtypeStruct((n_mod, nb, n_out), F32),
        compiler_params=_cparams(2),
        name="adaln",
    )(c.T, w, b)


def _proj0_kernel(x_ref, g_ref, sh_ref, sc_ref, w_ref, rc_ref, rs_ref, cc_ref, cs_ref, c0_ref, s0_ref, cj_ref, sj_ref,
                  gain_ref, gmat_ref, qa_ref, ka_ref, vt_ref, *rest):
    b_refs, stage_ref = rest[:-1], rest[-1]
    tm = x_ref.shape[1]
    col_c, col_s = cc_ref[...], cs_ref[...]
    ca = jnp.concatenate([rc_ref[r:r + 1, :] + col_c for r in range(tm // GRID_W)], axis=0)
    sa = jnp.concatenate([rs_ref[r:r + 1, :] + col_s for r in range(tm // GRID_W)], axis=0)
    c0, s0, cj, sj = c0_ref[0], s0_ref[0], cj_ref[...], sj_ref[...]
    lane = lax.broadcasted_iota(jnp.int32, (tm, LANES), 1)
    cb = c0 * cj - s0 * sj
    sb = jnp.where((lane & (HEAD_DIM // 2)) == 0, -1.0, 1.0) * (s0 * cj + c0 * sj)
    x = x_ref[0]
    h = _rms(x) * g_ref[...] * (1.0 + sc_ref[0]) + sh_ref[0]
    hb = h.astype(BF16)

    pa = _dot(hb, w_ref[:, 0:AB_A_W])
    n_qk = A_Q_W + A_KV_W
    qk = pa[:, 0:n_qk]
    sq = qk * qk
    hi = sq.astype(BF16)
    lo = (sq - hi.astype(F32)).astype(BF16)
    ms = (_dot(hi, gmat_ref[...]) + _dot(lo, gmat_ref[...])) * (1.0 / HEAD_DIM)
    qk = qk * lax.rsqrt(ms + RMS_EPS) * gain_ref[...]
    for j in range(n_qk // LANES):
        slab = _rope_slab(qk[:, j * LANES:(j + 1) * LANES], ca, sa, HEAD_DIM // 4).astype(BF16)
        if j < A_Q_W // LANES:
            qa_ref[0, :, j * LANES:(j + 1) * LANES] = slab
        else:
            ka_ref[0] = slab
    vt = pa[:, n_qk:AB_A_W].T
    ones = jnp.ones((A_V_ROWS - HEAD_DIM, A_TK), BF16)
    for kvh in range(A_KV_HEADS):
        for c in range(tm // A_TK):
            vt_ref[0, kvh, c, 0:HEAD_DIM, :] = vt[kvh * HEAD_DIM:(kvh + 1) * HEAD_DIM,
                                                  c * A_TK:(c + 1) * A_TK].astype(BF16)
            vt_ref[0, kvh, c, HEAD_DIM:A_V_ROWS, :] = ones

    pb = _dot(hb, w_ref[:, AB_A_W:AB_A_W + 3 * B_W])
    slot = 0
    for kind in range(3):
        for g, (_, dil) in enumerate(B_GROUPS):
            out_ref = b_refs[3 * g + kind]
            for hp in range(B_GROUP_W // LANES):
                lanes = slice(hp * LANES, (hp + 1) * LANES)
                col = kind * B_W + g * B_GROUP_W + hp * LANES
                slab = pb[:, col:col + LANES]
                if kind == 0:
                    slab = _rope_slab(slab, cb, sb, HEAD_DIM // 2) * (QK_SCALE * LOG2_E)
                elif kind == 1:
                    slab = _rope_slab(slab, cb, sb, HEAD_DIM // 2)
                if dil == 1:
                    out_ref[0, 0, :, lanes] = slab.astype(BF16)
                    continue
                stage_ref[slot] = slab
                for rho in range(dil):
                    out_ref[0, rho, :, lanes] = stage_ref[slot, pl.ds(rho, tm // dil, stride=dil), :].astype(BF16)
                slot += 1


def _proj0(x, g_pre, shift, scale, w_in, ropes, gain_qk, gmat):
    nb, s, d = x.shape
    tm = TOKEN_TILE
    tok = lambda b, i: (b, i, 0)
    mod = lambda b, i: (b, 0, 0)
    row_spec = pl.BlockSpec((tm // GRID_W, LANES), lambda b, i: (i, 0))
    col_spec = pl.BlockSpec((GRID_W, LANES), lambda b, i: (0, 0))
    base_spec = pl.BlockSpec((1, 1, LANES), lambda b, i: (i, 0, 0))
    in_tile_spec = pl.BlockSpec((tm, LANES), lambda b, i: (0, 0))
    n_qk = A_Q_W + A_KV_W
    out_shape = [
        jax.ShapeDtypeStruct((nb, s, A_Q_W), BF16),
        jax.ShapeDtypeStruct((nb, s, A_K_W), BF16),
        jax.ShapeDtypeStruct((nb, A_KV_HEADS, s // A_TK, A_V_ROWS, A_TK), BF16),
    ]
    out_specs = [
        pl.BlockSpec((1, tm, A_Q_W), tok),
        pl.BlockSpec((1, tm, A_K_W), tok),
        pl.BlockSpec((1, A_KV_HEADS, tm // A_TK, A_V_ROWS, A_TK), lambda b, i: (b, 0, i, 0, 0)),
    ]
    n_staged = 0
    for _, dil in B_GROUPS:
        for _ in range(3):
            out_shape.append(jax.ShapeDtypeStruct((nb, dil, s // dil, B_GROUP_W), BF16))
            out_specs.append(pl.BlockSpec((1, dil, tm // dil, B_GROUP_W), lambda b, i: (b, 0, i, 0)))
        n_staged += 3 * (B_GROUP_W // LANES) * (dil > 1)
    outs = pl.pallas_call(
        _proj0_kernel,
        grid=(nb, s // tm),
        in_specs=[pl.BlockSpec((1, tm, d), tok),
                  pl.BlockSpec((1, d), lambda b, i: (0, 0)),
                  pl.BlockSpec((1, 1, d), mod),
                  pl.BlockSpec((1, 1, d), mod),
                  _resident(w_in.shape, lambda b, i: (0, 0)),
                  row_spec, row_spec, col_spec, col_spec, base_spec, base_spec, in_tile_spec, in_tile_spec,
                  pl.BlockSpec((1, n_qk), lambda b, i: (0, 0)),
                  _resident((n_qk, n_qk), lambda b, i: (0, 0))],
        out_specs=out_specs,
        out_shape=out_shape,
        scratch_shapes=[pltpu.VMEM((n_staged, tm, LANES), F32)],
        compiler_params=_cparams(2),
        name="proj0",
    )(x, g_pre, shift, scale, w_in, *ropes, gain_qk, gmat)
    qa, ka, vt = outs[:3]
    groups = [tuple(outs[3 + 3 * g:6 + 3 * g]) for g in range(len(B_GROUPS))]
    return qa, ka, vt, groups


def _attn_a_kernel(q_ref, k_ref, vt_ref, gsum_ref, o_ref, qbd_ref, s_ref, kmax_ref):
    tq = q_ref.shape[1]
    n_chunks = k_ref.shape[1] // A_TK
    group = A_Q_HEADS // A_KV_HEADS
    ring = s_ref.shape[0]

    @pl.when(pl.program_id(1) == 0)
    def _():
        def longest(c, best):
            kk = k_ref[0, pl.ds(pl.multiple_of(c * A_TK, A_TK), A_TK), 0:A_KV_W].astype(F32)
            norms = _dot((kk * kk).astype(BF16), gsum_ref[...])
            return jnp.maximum(best, jnp.max(norms, axis=0, keepdims=True))
        kmax_ref[...] = lax.fori_loop(0, n_chunks, longest, jnp.zeros((1, A_KV_W), F32))

    qt = q_ref[0].astype(F32).T
    zeros = jnp.zeros((HEAD_DIM, tq), F32)
    shifts = []
    for h in range(A_Q_HEADS):
        kvh = h // group
        blk = qt[h * HEAD_DIM:(h + 1) * HEAD_DIM]
        shifts.append(jnp.sqrt(jnp.sum(blk * blk, axis=0, keepdims=True)
                               * kmax_ref[:, kvh * HEAD_DIM:kvh * HEAD_DIM + 1]))
        full = jnp.concatenate([blk, zeros] if kvh == 0 else [zeros, blk], axis=0)
        qbd_ref[h] = full.astype(BF16)

    def issue_score(slot, chunk, h):
        k = k_ref[0, pl.ds(pl.multiple_of(chunk * A_TK, A_TK), A_TK), :]
        s_ref[slot] = _dot(k, qbd_ref[h])

    def attend(unroll, shifted):
        strips = [(u, h) for u in range(unroll) for h in range(A_Q_HEADS)]
        n_strips = len(strips)
        assert n_strips % ring == 0 and A_LOOKAHEAD < ring and n_chunks % unroll == 0

        def body(c, carry):
            ms, accs = list(carry[0]), list(carry[1])
            next_base = jnp.minimum(c + 1, n_chunks // unroll - 1) * unroll
            for idx, (u, h) in enumerate(strips):
                nxt = idx + A_LOOKAHEAD
                if nxt < n_strips:
                    issue_score(nxt % ring, c * unroll + strips[nxt][0], strips[nxt][1])
                else:
                    u2, h2 = strips[nxt - n_strips]
                    issue_score(nxt % ring, next_base + u2, h2)
                slot = idx % ring
                vt = vt_ref[0, h // group, c * unroll + u]
                if shifted:
                    accs[h] = accs[h] + _dot(vt, jnp.exp2(s_ref[slot] - shifts[h]).astype(BF16))
                    continue
                m_new = jnp.maximum(ms[h], jnp.max(s_ref[slot], axis=0, keepdims=True))
                alpha = jnp.exp2(ms[h] - m_new)
                p = jnp.exp2(s_ref[slot] - m_new).astype(BF16)
                ms[h] = m_new
                accs[h] = accs[h] * alpha + _dot(vt, p)
            return tuple(ms), tuple(accs)

        for idx in range(A_LOOKAHEAD):
            issue_score(idx, strips[idx][0], strips[idx][1])
        init = (tuple(jnp.full((1, tq), -jnp.inf, F32) for _ in range(0 if shifted else A_Q_HEADS)),
                tuple(jnp.zeros((A_V_ROWS, tq), F32) for _ in range(A_Q_HEADS)))
        return lax.fori_loop(0, n_chunks // unroll, body, init)[1]

    def finish(accs):
        outs = [a[0:HEAD_DIM] / a[HEAD_DIM:HEAD_DIM + 1] for a in accs]
        o_ref[0] = jnp.concatenate(outs, axis=0).T.astype(BF16)

    accs = attend(A_UNROLL, True)
    finish(accs)
    smallest = jnp.min(functools.reduce(jnp.minimum, [a[HEAD_DIM:HEAD_DIM + 1] for a in accs]))

    @pl.when(jnp.logical_not(smallest > A_MIN_DENOM))
    def _():
        finish(attend(A_SAFE_UNROLL, False))


def _attn_a(qa, ka, vt):
    nb, s, _ = qa.shape
    tq = A_TQ
    gsum = _head_sum_matrix(A_KV_W)
    return pl.pallas_call(
        _attn_a_kernel,
        grid=(nb, s // tq),
        in_specs=[pl.BlockSpec((1, tq, A_Q_W), lambda b, i: (b, i, 0)),
                  _resident((1, s, A_K_W), lambda b, i: (b, 0, 0)),
                  _resident((1,) + vt.shape[1:], lambda b, i: (b, 0, 0, 0, 0)),
                  _resident(gsum.shape, lambda b, i: (0, 0))],
        out_specs=pl.BlockSpec((1, tq, A_Q_W), lambda b, i: (b, i, 0)),
        out_shape=jax.ShapeDtypeStruct((nb, s, A_Q_W), BF16),
        scratch_shapes=[pltpu.VMEM((A_Q_HEADS, A_K_W, tq), BF16),
                        pltpu.VMEM((A_SCORE_RING, A_TK, tq), F32),
                        pltpu.VMEM((1, A_KV_W), F32)],
        compiler_params=_cparams(2),
        name="attn_a",
    )(qa, ka, vt, gsum)


def _left_lanes(shape):
    return lax.broadcasted_iota(jnp.int32, shape, 1) < HEAD_DIM


def _head_scores(q, kw, hd):
    left = _left_lanes(q.shape)
    zero = jnp.zeros_like(q)
    qh = jnp.where(left, q, zero) if hd == 0 else jnp.where(left, zero, q)
    return _dot_nt(qh, kw)


def _softmax_pv(s, vw):
    m = jnp.max(s, axis=-1, keepdims=True)
    p = jnp.exp(s - m)
    l = jnp.sum(p, axis=-1, keepdims=True)
    o = _dot(p.astype(BF16), vw) * (1.0 / l)
    return o, m + jnp.log(l)


def _run_tasks(tasks, lookahead, score, finish):
    pending = [score(t) for t in tasks[:lookahead]]
    for idx, t in enumerate(tasks):
        if idx + lookahead < len(tasks):
            pending.append(score(tasks[idx + lookahead]))
        finish(t, pending.pop(0))


def _attn_b_kernel(*refs):
    n_g = len(B_GROUPS)
    q_refs, k_refs, v_refs = refs[0:n_g], refs[n_g:2 * n_g], refs[2 * n_g:3 * n_g]
    bias_ref, o_ref = refs[3 * n_g], refs[3 * n_g + 1]
    o_scr, l_scr = refs[3 * n_g + 2:4 * n_g + 2], refs[4 * n_g + 2:5 * n_g + 2]
    tile_i = pl.program_id(2)
    left = _left_lanes((B_TQ, LANES))
    n_sub = B_TILE // B_TQ
    win = B_TQ + 2 * B_HALF

    def body(it, carry):
        geo = {}
        for j in range(B_SUBS_PER_ITER):
            k = it * B_SUBS_PER_ITER + j
            for g, (_, dil) in enumerate(B_GROUPS):
                per_res = n_sub // dil
                shift = per_res.bit_length() - 1
                rho = lax.shift_right_logical(k, jnp.int32(shift))
                sub = k & (per_res - 1)
                u0 = tile_i * (B_TILE // dil) + sub * B_TQ
                ws = jnp.clip(u0 - B_HALF, 0, k_refs[g].shape[2] - win)
                var = 1 - lax.shift_right_arithmetic(ws - (u0 - B_HALF), jnp.int32(B_HALF.bit_length() - 1))
                geo[(j, g)] = (rho, sub, pl.multiple_of(ws, B_HALF), var, dil)

        def score(task):
            j, g, hd = task
            rho, sub, ws, var, _ = geo[(j, g)]
            q = q_refs[g][0, rho, pl.ds(pl.multiple_of(sub * B_TQ, B_TQ), B_TQ), :]
            return _head_scores(q, k_refs[g][0, rho, pl.ds(ws, win), :], hd) + bias_ref[var]

        first = {}

        def finish(task, s):
            j, g, hd = task
            rho, sub, ws, _, dil = geo[(j, g)]
            m = jnp.max(s, axis=-1, keepdims=True)
            p = jnp.exp2(s - m)
            l = jnp.sum(p, axis=-1, keepdims=True)
            o = _dot(p.astype(BF16), v_refs[g][0, rho, pl.ds(ws, win), :]) * (1.0 / l)
            lse = m + jnp.log2(l)
            if hd == 0:
                first[(j, g)] = (o, lse)
                return
            o0, lse0 = first.pop((j, g))
            row0 = rho + (dil * B_TQ) * sub
            rows = pl.ds(row0, B_TQ, stride=dil) if dil > 1 else pl.ds(pl.multiple_of(row0, B_TQ), B_TQ)
            o_scr[g][rows, :] = jnp.where(left, o0, o)
            l_scr[g][rows, :] = jnp.where(left, lse0, lse)

        tasks = [(j, g, hd) for j in range(B_SUBS_PER_ITER) for g in range(n_g) for hd in range(2)]
        _run_tasks(tasks, B_LOOKAHEAD, score, finish)
        return carry

    lax.fori_loop(0, n_sub // B_SUBS_PER_ITER, body, 0)
    lses = [r[...] for r in l_scr]
    m = functools.reduce(jnp.maximum, lses)
    es = [jnp.exp2(l - m) for l in lses]
    num = functools.reduce(lambda a, b: a + b, [e * r[...] for e, r in zip(es, o_scr)])
    o_ref[0] = (num / functools.reduce(lambda a, b: a + b, es)).astype(BF16)


def _attn_b(groups):
    nb, _, s, _ = groups[0][0].shape
    assert s % B_TILE == 0 and all(s // dil >= B_TQ + 2 * B_HALF for _, dil in B_GROUPS)
    assert all(window // (2 * dil) == B_HALF and (B_TILE // B_TQ) % dil == 0 for window, dil in B_GROUPS)
    win = B_TQ + 2 * B_HALF
    shift = jnp.asarray([B_HALF, 0, -B_HALF], jnp.int32)[:, None, None]
    du = (jnp.arange(win, dtype=jnp.int32)[None, None, :] - jnp.arange(B_TQ, dtype=jnp.int32)[None, :, None]
          + shift - B_HALF)
    bias = jnp.where(jnp.abs(du) <= B_HALF, 0.0, NEG_BIG).astype(F32)
    q_specs, kv_specs = [], []
    for _, dil in B_GROUPS:
        q_specs.append(pl.BlockSpec((1, dil, B_TILE // dil, LANES), lambda b, hp, i: (b, 0, i, hp)))
        kv_specs.append(_resident((1, dil, s // dil, LANES), lambda b, hp, i: (b, 0, 0, hp)))
    qs, ks, vs = zip(*groups)
    return pl.pallas_call(
        _attn_b_kernel,
        grid=(nb, B_GROUP_W // LANES, s // B_TILE),
        in_specs=q_specs + kv_specs + kv_specs + [_resident(bias.shape, lambda b, hp, i: (0, 0, 0))],
        out_specs=pl.BlockSpec((1, B_TILE, LANES), lambda b, hp, i: (b, i, hp)),
        out_shape=jax.ShapeDtypeStruct((nb, s, B_GROUP_W), BF16),
        scratch_shapes=[pltpu.VMEM((B_TILE, LANES), F32) for _ in range(2 * len(B_GROUPS))],
        compiler_params=_cparams(3),
        name="attn_b",
    )(*qs, *ks, *vs, bias)


def _finish(y, x, g_post, gate):
    return x + gate * (_rms(y) * g_post)


def _mix_mlp_kernel(*refs, n_parts):
    parts, weights = refs[:n_parts], refs[n_parts:2 * n_parts]
    (x_ref, gmix_ref, gate_mix_ref, gpre_ref, sh_ref, sc_ref, wu_ref, wd_ref, gpost_ref, gate_ref,
     y_ref) = refs[2 * n_parts:]
    y = _dot(parts[0][0], weights[0][...])
    for o_ref, w_ref in zip(parts[1:], weights[1:]):
        y = y + _dot(o_ref[0], w_ref[...])
    x = _finish(y, x_ref[0], gmix_ref[...], gate_mix_ref[0])
    hb = (_rms(x) * gpre_ref[...] * (1.0 + sc_ref[0]) + sh_ref[0]).astype(BF16)
    d = x.shape[-1]
    acc = jnp.zeros(x.shape, F32)
    for j in range(wu_ref.shape[1] // d):
        u = jnp.maximum(_dot(hb, wu_ref[:, j * d:(j + 1) * d]), 0.0)
        acc = acc + _dot((u * u).astype(BF16), wd_ref[j * d:(j + 1) * d, :])
    y_ref[0] = _finish(acc, x, gpost_ref[...], gate_ref[0])


def _mix_mlp(parts, w_outs, x, g_mix, gate_mix, g_pre, shift, scale, w_up, w_down, g_post, gate):
    nb, s, d = x.shape
    tm = TOKEN_TILE
    tok = lambda b, i: (b, i, 0)
    vec = pl.BlockSpec((1, d), lambda b, i: (0, 0))
    mod = pl.BlockSpec((1, 1, d), lambda b, i: (b, 0, 0))
    const = lambda a: _resident(a.shape, lambda b, i: (0, 0))
    return pl.pallas_call(
        functools.partial(_mix_mlp_kernel, n_parts=len(parts)),
        grid=(nb, s // tm),
        in_specs=[pl.BlockSpec((1, tm, p.shape[-1]), tok) for p in parts] + [const(w) for w in w_outs] +
                 [pl.BlockSpec((1, tm, d), tok), vec, mod, vec, mod, mod, const(w_up), const(w_down), vec, mod],
        out_specs=pl.BlockSpec((1, tm, d), tok),
        out_shape=jax.ShapeDtypeStruct(x.shape, F32),
        compiler_params=_cparams(2),
        name="mix_mlp",
    )(*parts, *w_outs, x, g_mix, gate_mix, g_pre, shift, scale, w_up, w_down, g_post, gate)


def _proj1_kernel(x_ref, g_ref, sh_ref, sc_ref, w_ref, q_ref, k_ref, v_ref):
    hb = (_rms(x_ref[0]) * g_ref[...] * (1.0 + sc_ref[0]) + sh_ref[0]).astype(BF16)
    d = q_ref.shape[-1]
    q_ref[0] = (_dot(hb, w_ref[:, 0:d]) * QK_SCALE).astype(BF16)
    k_ref[0] = _dot(hb, w_ref[:, d:2 * d]).astype(BF16)
    v_ref[0] = _dot(hb, w_ref[:, 2 * d:3 * d]).astype(BF16)


def _proj1(x, g_pre, shift, scale, w_in):
    nb, s, d = x.shape
    tm = TOKEN_TILE
    n = w_in.shape[1] // 3
    tok = lambda b, i: (b, i, 0)
    mod = pl.BlockSpec((1, 1, d), lambda b, i: (b, 0, 0))
    out = jax.ShapeDtypeStruct((nb, s, n), BF16)
    return pl.pallas_call(
        _proj1_kernel,
        grid=(nb, s // tm),
        in_specs=[pl.BlockSpec((1, tm, d), tok),
                  pl.BlockSpec((1, d), lambda b, i: (0, 0)), mod, mod,
                  _resident(w_in.shape, lambda b, i: (0, 0))],
        out_specs=(pl.BlockSpec((1, tm, n), tok),) * 3,
        out_shape=(out, out, out),
        compiler_params=_cparams(2),
        name="proj1",
    )(x, g_pre, shift, scale, w_in)


def _attn_c_kernel(q_ref, k_ref, v_ref, bias_ref, o_ref):
    rows = k_ref.shape[1] // GRID_W
    pair_tokens = 2 * GRID_W
    win = C_WIN_ROWS * GRID_W
    blk = pl.program_id(2)
    left = _left_lanes((pair_tokens, LANES))
    starts, variants = [], []
    for j in range(C_PAIRS):
        nominal = 2 * (blk * C_PAIRS + j) - NA_ROWS // 2
        ws = jnp.clip(nominal, 0, rows - C_WIN_ROWS)
        variants.append(2 - lax.shift_right_arithmetic(ws - nominal, 1))
        starts.append(pl.multiple_of(ws * GRID_W, GRID_W))
    toks = lambda j: slice(j * pair_tokens, (j + 1) * pair_tokens)

    def score(task):
        j, hd = task
        return _head_scores(q_ref[0, toks(j), :], k_ref[0, pl.ds(starts[j], win), :], hd) + bias_ref[hd, variants[j]]

    first = {}

    def finish(task, s):
        j, hd = task
        o, _ = _softmax_pv(s, v_ref[0, pl.ds(starts[j], win), :])
        if hd == 0:
            first[j] = o
        else:
            o_ref[0, toks(j), :] = jnp.where(left, first.pop(j), o).astype(BF16)

    _run_tasks([(j, hd) for j in range(C_PAIRS) for hd in range(2)], C_LOOKAHEAD, score, finish)


def _na_bias_table(rpb, rows):
    n_heads, n_dr, n_dc = rpb.shape
    n_pairs = rows // 2
    c = np.arange(GRID_W)
    cs = np.clip(c - NA_COLS // 2, 0, GRID_W - NA_COLS)
    kc = np.arange(GRID_W)
    valid_c = (kc[None, :] >= cs[:, None]) & (kc[None, :] < cs[:, None] + NA_COLS)
    span = 2 * GRID_W - 1
    side = (span - n_dc) // 2
    g = jnp.pad(rpb.astype(F32), ((0, 0), (0, 0), (side, side)), constant_values=NEG_BIG)
    flat = jnp.tile(g, (1, 1, GRID_W))
    toep = flat[:, :, GRID_W - 1:GRID_W - 1 + GRID_W * (span - 1)]
    toep = toep.reshape(n_heads, n_dr, GRID_W, span - 1)[..., :GRID_W]
    toep = jnp.where(valid_c[None, None], toep, NEG_BIG)
    masked = jnp.full((n_heads, GRID_W, GRID_W), NEG_BIG, F32)
    tiles = []
    for i in (0, 1, 2, n_pairs - 2, n_pairs - 1):
        ws = int(np.clip(2 * i - NA_ROWS // 2, 0, rows - C_WIN_ROWS))
        halves = []
        for qr in range(2):
            r = 2 * i + qr
            rs = int(np.clip(r - NA_ROWS // 2, 0, rows - NA_ROWS))
            blocks = []
            for kr in range(ws, ws + C_WIN_ROWS):
                blocks.append(toep[:, kr - r + NA_ROWS - 1] if rs <= kr < rs + NA_ROWS else masked)
            halves.append(jnp.concatenate(blocks, axis=-1))
        tiles.append(jnp.concatenate(halves, axis=1))
    return jnp.stack(tiles, axis=1)


def _attn_c(q, k, v, bias):
    nb, s, w = q.shape
    pairs = w // LANES
    step_tokens = C_PAIRS * 2 * GRID_W
    n_var, tq, win = bias.shape[1:]
    col = lambda b, hp, i: (b, 0, hp)
    return pl.pallas_call(
        _attn_c_kernel,
        grid=(nb, pairs, s // step_tokens),
        in_specs=[pl.BlockSpec((1, step_tokens, LANES), lambda b, hp, i: (b, i, hp)),
                  pl.BlockSpec((1, s, LANES), col),
                  pl.BlockSpec((1, s, LANES), col),
                  _resident((2, n_var, tq, win), lambda b, hp, i: (hp, 0, 0, 0))],
        out_specs=pl.BlockSpec((1, step_tokens, LANES), lambda b, hp, i: (b, i, hp)),
        out_shape=jax.ShapeDtypeStruct((nb, s, w), BF16),
        compiler_params=_cparams(3),
        name="attn_c",
    )(q, k, v, bias)


def _rope_parts(s, tm):
    def cos_sin(pos, dim):
        inv_freq = ROPE_THETA ** (-jnp.arange(0, dim, 2, dtype=F32) / dim)
        ang = pos.astype(F32)[:, None] * inv_freq[None, :]
        return jnp.cos(ang), jnp.sin(ang)

    rows = s // GRID_W
    cr, sr = cos_sin(jnp.arange(rows, dtype=jnp.int32), HEAD_DIM // 2)
    cc, sc = cos_sin(jnp.arange(GRID_W, dtype=jnp.int32), HEAD_DIM // 2)
    zr, zc = jnp.zeros_like(cr), jnp.zeros_like(cc)
    axial = (jnp.concatenate([cr, cr, zr, zr] * 2, axis=-1), jnp.concatenate([-sr, sr, zr, zr] * 2, axis=-1),
             jnp.concatenate([zc, zc, cc, cc] * 2, axis=-1), jnp.concatenate([zc, zc, -sc, sc] * 2, axis=-1))
    c0, s0 = cos_sin(jnp.arange(0, s, tm, dtype=jnp.int32), HEAD_DIM)
    cj, sj = cos_sin(jnp.arange(tm, dtype=jnp.int32), HEAD_DIM)
    wide = lambda a: jnp.concatenate([a] * 4, axis=-1)
    linear = (wide(c0)[:, None, :], wide(s0)[:, None, :], wide(cj), wide(sj))
    return axial + linear


def _head_sum_matrix(n):
    idx = np.arange(n) // HEAD_DIM
    return jnp.asarray(idx[:, None] == idx[None, :], BF16)


def kernel(x, c, ada_w, ada_b, norm_g, ab_w_in, ab_w_out, a_q_gain, a_k_gain,
           c_w_in, c_w_out, c_rpb, mlp_w_up, mlp_w_down):
    nb, s, d = x.shape
    assert s % (C_PAIRS * 2 * GRID_W) == 0 and s // GRID_W >= 2 * C_WIN_ROWS
    mods = _adaln(c, ada_w, ada_b).reshape(ada_w.shape[0], 2, nb, 3, d)

    def mod(layer, which):
        m = mods[layer, which]
        return m[:, 0:1], m[:, 1:2], m[:, 2:3]

    ropes = _rope_parts(s, TOKEN_TILE)
    gain_qk = jnp.concatenate([jnp.tile(a_q_gain[0] * (QK_SCALE * LOG2_E), A_Q_HEADS),
                               jnp.tile(a_k_gain[0], A_KV_HEADS)])[None, :]
    gmat = _head_sum_matrix(A_Q_W + A_KV_W)

    shift, scale, gate = mod(0, 0)
    qa, ka, vt, groups = _proj0(x, norm_g[0, 0:1], shift, scale, ab_w_in[0].astype(BF16), ropes, gain_qk, gmat)
    oa = _attn_a(qa, ka, vt)
    ob = _attn_b(groups)
    w_out = ab_w_out[0].astype(BF16)
    shift, scale, gate_mlp = mod(0, 1)
    x = _mix_mlp([oa, ob], [w_out[:A_Q_W], w_out[A_Q_W:]], x, norm_g[0, 1:2], gate,
                 norm_g[0, 2:3], shift, scale, mlp_w_up[0].astype(BF16), mlp_w_down[0].astype(BF16),
                 norm_g[0, 3:4], gate_mlp)

    shift, scale, gate = mod(1, 0)
    q, k, v = _proj1(x, norm_g[1, 0:1], shift, scale, c_w_in[0].astype(BF16))
    oc = _attn_c(q, k, v, _na_bias_table(c_rpb[0], s // GRID_W))
    shift, scale, gate_mlp = mod(1, 1)
    return _mix_mlp([oc], [c_w_out[0].astype(BF16)], x, norm_g[1, 1:2], gate,
                    norm_g[1, 2:3], shift, scale, mlp_w_up[1].astype(BF16), mlp_w_down[1].astype(BF16),
                    norm_g[1, 3:4], gate_mlp)
```

```python
import functools
import math

import numpy as np
import jax
import jax.numpy as jnp
from jax import lax
from jax.experimental import pallas as pl
from jax.experimental.pallas import tpu as pltpu

F32 = jnp.float32
BF16 = jnp.bfloat16

HEAD_DIM = 64
GRID_W = 64
ROPE_THETA = 10000.0
RMS_EPS = 1e-6
A_Q_HEADS = 8
A_KV_HEADS = 2
B_GROUPS = ((128, 1), (512, 4), (2048, 16))
B_HEADS_PER_GROUP = 4
C_HEADS = 16
NA_ROWS = 8
NA_COLS = 16
A_Q_W = A_Q_HEADS * HEAD_DIM
A_KV_W = A_KV_HEADS * HEAD_DIM
B_GROUP_W = B_HEADS_PER_GROUP * HEAD_DIM
B_W = len(B_GROUPS) * B_GROUP_W
AB_A_W = A_Q_W + 2 * A_KV_W
QK_SCALE = 1.0 / math.sqrt(HEAD_DIM)
LOG2_E = math.log2(math.e)

LANES = 128
V7X_VMEM_LIMIT_BYTES = 56 * 1024 * 1024
NEG_BIG = -1e30

TOKEN_TILE = 512
A_TQ = 256
A_TK = 256
A_K_W = A_KV_W
A_UNROLL = 16
A_SAFE_UNROLL = 2
A_MIN_DENOM = 2.0 ** -60
A_LOOKAHEAD = 4
A_SCORE_RING = 8
A_V_ROWS = 80
B_TILE = 2048
B_TQ = 128
B_HALF = 64
B_SUBS_PER_ITER = 4
B_LOOKAHEAD = 4
C_LOOKAHEAD = 3
C_PAIRS = 16
C_WIN_ROWS = 10


def _cparams(n_axes):
    return pltpu.CompilerParams(dimension_semantics=("arbitrary",) * n_axes,
                                vmem_limit_bytes=V7X_VMEM_LIMIT_BYTES)


def _resident(block_shape, index_map):
    return pl.BlockSpec(block_shape, index_map, pipeline_mode=pl.Buffered(1))


def _rms(x):
    return x * lax.rsqrt(jnp.mean(x * x, axis=-1, keepdims=True) + RMS_EPS)


def _dot(a, b):
    return jnp.dot(a, b, preferred_element_type=F32)


def _dot_nt(a, b):
    return lax.dot_general(a, b, (((1,), (1,)), ((), ())), preferred_element_type=F32)


def _rope_slab(x, cos, sin_signed, half):
    lane = lax.broadcasted_iota(jnp.int32, x.shape, 1)
    first = (lane & half) == 0
    rot = jnp.where(first, pltpu.roll(x, LANES - half, 1), pltpu.roll(x, half, 1))
    return x * cos + rot * sin_signed


def _adaln_kernel(ct_ref, w_ref, b_ref, o_ref):
    ct = ct_ref[...]
    cond = ct * jax.nn.sigmoid(ct)
    w = w_ref[0]
    rows = [jnp.sum(w * cond[:, b:b + 1], axis=0, keepdims=True) for b in range(ct.shape[1])]
    o_ref[0] = jnp.concatenate(rows, axis=0) + b_ref[0]


def _adaln(c, ada_w, ada_b):
    nb, d = c.shape
    n_mod = ada_w.shape[0] * ada_w.shape[1]
    n_out = ada_w.shape[-1]
    tn = 768
    w = ada_w.reshape(n_mod, d, n_out)
    b = ada_b.reshape(n_mod, 1, n_out)
    return pl.pallas_call(
        _adaln_kernel,
        grid=(n_mod, n_out // tn),
        in_specs=[pl.BlockSpec((d, nb), lambda m, j: (0, 0)),
                  pl.BlockSpec((1, d, tn), lambda m, j: (m, 0, j)),
                  pl.BlockSpec((1, 1, tn), lambda m, j: (m, 0, j))],
        out_specs=pl.BlockSpec((1, nb, tn), lambda m, j: (m, 0, j)),
        out_shape=jax.ShapeDtypeStruct((n_mod, nb, n_out), F32),
        compiler_params=_cparams(2),
        name="adaln",
    )(c.T, w, b)


def _proj0_kernel(x_ref, g_ref, sh_ref, sc_ref, w_ref, rc_ref, rs_ref, cc_ref, cs_ref, c0_ref, s0_ref, cj_ref, sj_ref,
                  gain_ref, gmat_ref, qa_ref, ka_ref, vt_ref, *rest):
    b_refs, stage_ref = rest[:-1], rest[-1]
    tm = x_ref.shape[1]
    col_c, col_s = cc_ref[...], cs_ref[...]
    ca = jnp.concatenate([rc_ref[r:r + 1, :] + col_c for r in range(tm // GRID_W)], axis=0)
    sa = jnp.concatenate([rs_ref[r:r + 1, :] + col_s for r in range(tm // GRID_W)], axis=0)
    c0, s0, cj, sj = c0_ref[0], s0_ref[0], cj_ref[...], sj_ref[...]
    lane = lax.broadcasted_iota(jnp.int32, (tm, LANES), 1)
    cb = c0 * cj - s0 * sj
    sb = jnp.where((lane & (HEAD_DIM // 2)) == 0, -1.0, 1.0) * (s0 * cj + c0 * sj)
    x = x_ref[0]
    h = _rms(x) * g_ref[...] * (1.0 + sc_ref[0]) + sh_ref[0]
    hb = h.astype(BF16)

    pa = _dot(hb, w_ref[:, 0:AB_A_W])
    n_qk = A_Q_W + A_KV_W
    qk = pa[:, 0:n_qk]
    sq = qk * qk
    hi = sq.astype(BF16)
    lo = (sq - hi.astype(F32)).astype(BF16)
    ms = (_dot(hi, gmat_ref[...]) + _dot(lo, gmat_ref[...])) * (1.0 / HEAD_DIM)
    qk = qk * lax.rsqrt(ms + RMS_EPS) * gain_ref[...]
    for j in range(n_qk // LANES):
        slab = _rope_slab(qk[:, j * LANES:(j + 1) * LANES], ca, sa, HEAD_DIM // 4).astype(BF16)
        if j < A_Q_W // LANES:
            qa_ref[0, :, j * LANES:(j + 1) * LANES] = slab
        else:
            ka_ref[0] = slab
    vt = pa[:, n_qk:AB_A_W].T
    ones = jnp.ones((A_V_ROWS - HEAD_DIM, A_TK), BF16)
    for kvh in range(A_KV_HEADS):
        for c in range(tm // A_TK):
            vt_ref[0, kvh, c, 0:HEAD_DIM, :] = vt[kvh * HEAD_DIM:(kvh + 1) * HEAD_DIM,
                                                  c * A_TK:(c + 1) * A_TK].astype(BF16)
            vt_ref[0, kvh, c, HEAD_DIM:A_V_ROWS, :] = ones

    pb = _dot(hb, w_ref[:, AB_A_W:AB_A_W + 3 * B_W])
    slot = 0
    for kind in range(3):
        for g, (_, dil) in enumerate(B_GROUPS):
            out_ref = b_refs[3 * g + kind]
            for hp in range(B_GROUP_W // LANES):
                lanes = slice(hp * LANES, (hp + 1) * LANES)
                col = kind * B_W + g * B_GROUP_W + hp * LANES
                slab = pb[:, col:col + LANES]
                if kind == 0:
                    slab = _rope_slab(slab, cb, sb, HEAD_DIM // 2) * (QK_SCALE * LOG2_E)
                elif kind == 1:
                    slab = _rope_slab(slab, cb, sb, HEAD_DIM // 2)
                if dil == 1:
                    out_ref[0, 0, :, lanes] = slab.astype(BF16)
                    continue
                stage_ref[slot] = slab
                for rho in range(dil):
                    out_ref[0, rho, :, lanes] = stage_ref[slot, pl.ds(rho, tm // dil, stride=dil), :].astype(BF16)
                slot += 1


def _proj0(x, g_pre, shift, scale, w_in, ropes, gain_qk, gmat):
    nb, s, d = x.shape
    tm = TOKEN_TILE
    tok = lambda b, i: (b, i, 0)
    mod = lambda b, i: (b, 0, 0)
    row_spec = pl.BlockSpec((tm // GRID_W, LANES), lambda b, i: (i, 0))
    col_spec = pl.BlockSpec((GRID_W, LANES), lambda b, i: (0, 0))
    base_spec = pl.BlockSpec((1, 1, LANES), lambda b, i: (i, 0, 0))
    in_tile_spec = pl.BlockSpec((tm, LANES), lambda b, i: (0, 0))
    n_qk = A_Q_W + A_KV_W
    out_shape = [
        jax.ShapeDtypeStruct((nb, s, A_Q_W), BF16),
        jax.ShapeDtypeStruct((nb, s, A_K_W), BF16),
        jax.ShapeDtypeStruct((nb, A_KV_HEADS, s // A_TK, A_V_ROWS, A_TK), BF16),
    ]
    out_specs = [
        pl.BlockSpec((1, tm, A_Q_W), tok),
        pl.BlockSpec((1, tm, A_K_W), tok),
        pl.BlockSpec((1, A_KV_HEADS, tm // A_TK, A_V_ROWS, A_TK), lambda b, i: (b, 0, i, 0, 0)),
    ]
    n_staged = 0
    for _, dil in B_GROUPS:
        for _ in range(3):
            out_shape.append(jax.ShapeDtypeStruct((nb, dil, s // dil, B_GROUP_W), BF16))
            out_specs.append(pl.BlockSpec((1, dil, tm // dil, B_GROUP_W), lambda b, i: (b, 0, i, 0)))
        n_staged += 3 * (B_GROUP_W // LANES) * (dil > 1)
    outs = pl.pallas_call(
        _proj0_kernel,
        grid=(nb, s // tm),
        in_specs=[pl.BlockSpec((1, tm, d), tok),
                  pl.BlockSpec((1, d), lambda b, i: (0, 0)),
                  pl.BlockSpec((1, 1, d), mod),
                  pl.BlockSpec((1, 1, d), mod),
                  _resident(w_in.shape, lambda b, i: (0, 0)),
                  row_spec, row_spec, col_spec, col_spec, base_spec, base_spec, in_tile_spec, in_tile_spec,
                  pl.BlockSpec((1, n_qk), lambda b, i: (0, 0)),
                  _resident((n_qk, n_qk), lambda b, i: (0, 0))],
        out_specs=out_specs,
        out_shape=out_shape,
        scratch_shapes=[pltpu.VMEM((n_staged, tm, LANES), F32)],
        compiler_params=_cparams(2),
        name="proj0",
    )(x, g_pre, shift, scale, w_in, *ropes, gain_qk, gmat)
    qa, ka, vt = outs[:3]
    groups = [tuple(outs[3 + 3 * g:6 + 3 * g]) for g in range(len(B_GROUPS))]
    return qa, ka, vt, groups


def _attn_a_kernel(q_ref, k_ref, vt_ref, gsum_ref, o_ref, qbd_ref, s_ref, kmax_ref):
    tq = q_ref.shape[1]
    n_chunks = k_ref.shape[1] // A_TK
    group = A_Q_HEADS // A_KV_HEADS
    ring = s_ref.shape[0]

    @pl.when(pl.program_id(1) == 0)
    def _():
        def longest(c, best):
            kk = k_ref[0, pl.ds(pl.multiple_of(c * A_TK, A_TK), A_TK), 0:A_KV_W].astype(F32)
            norms = _dot((kk * kk).astype(BF16), gsum_ref[...])
            return jnp.maximum(best, jnp.max(norms, axis=0, keepdims=True))
        kmax_ref[...] = lax.fori_loop(0, n_chunks, longest, jnp.zeros((1, A_KV_W), F32))

    qt = q_ref[0].astype(F32).T
    zeros = jnp.zeros((HEAD_DIM, tq), F32)
    shifts = []
    for h in range(A_Q_HEADS):
        kvh = h // group
        blk = qt[h * HEAD_DIM:(h + 1) * HEAD_DIM]
        shifts.append(jnp.sqrt(jnp.sum(blk * blk, axis=0, keepdims=True)
                               * kmax_ref[:, kvh * HEAD_DIM:kvh * HEAD_DIM + 1]))
        full = jnp.concatenate([blk, zeros] if kvh == 0 else [zeros, blk], axis=0)
        qbd_ref[h] = full.astype(BF16)

    def issue_score(slot, chunk, h):
        k = k_ref[0, pl.ds(pl.multiple_of(chunk * A_TK, A_TK), A_TK), :]
        s_ref[slot] = _dot(k, qbd_ref[h])

    def attend(unroll, shifted):
        strips = [(u, h) for u in range(unroll) for h in range(A_Q_HEADS)]
        n_strips = len(strips)
        assert n_strips % ring == 0 and A_LOOKAHEAD < ring and n_chunks % unroll == 0

        def body(c, carry):
            ms, accs = list(carry[0]), list(carry[1])
            next_base = jnp.minimum(c + 1, n_chunks // unroll - 1) * unroll
            for idx, (u, h) in enumerate(strips):
                nxt = idx + A_LOOKAHEAD
                if nxt < n_strips:
                    issue_score(nxt % ring, c * unroll + strips[nxt][0], strips[nxt][1])
                else:
                    u2, h2 = strips[nxt - n_strips]
                    issue_score(nxt % ring, next_base + u2, h2)
                slot = idx % ring
                vt = vt_ref[0, h // group, c * unroll + u]
                if shifted:
                    accs[h] = accs[h] + _dot(vt, jnp.exp2(s_ref[slot] - shifts[h]).astype(BF16))
                    continue
                m_new = jnp.maximum(ms[h], jnp.max(s_ref[slot], axis=0, keepdims=True))
                alpha = jnp.exp2(ms[h] - m_new)
                p = jnp.exp2(s_ref[slot] - m_new).astype(BF16)
                ms[h] = m_new
                accs[h] = accs[h] * alpha + _dot(vt, p)
            return tuple(ms), tuple(accs)

        for idx in range(A_LOOKAHEAD):
            issue_score(idx, strips[idx][0], strips[idx][1])
        init = (tuple(jnp.full((1, tq), -jnp.inf, F32) for _ in range(0 if shifted else A_Q_HEADS)),
                tuple(jnp.zeros((A_V_ROWS, tq), F32) for _ in range(A_Q_HEADS)))
        return lax.fori_loop(0, n_chunks // unroll, body, init)[1]

    def finish(accs):
        outs = [a[0:HEAD_DIM] / a[HEAD_DIM:HEAD_DIM + 1] for a in accs]
        o_ref[0] = jnp.concatenate(outs, axis=0).T.astype(BF16)

    accs = attend(A_UNROLL, True)
    finish(accs)
    smallest = jnp.min(functools.reduce(jnp.minimum, [a[HEAD_DIM:HEAD_DIM + 1] for a in accs]))

    @pl.when(jnp.logical_not(smallest > A_MIN_DENOM))
    def _():
        finish(attend(A_SAFE_UNROLL, False))


def _attn_a(qa, ka, vt):
    nb, s, _ = qa.shape
    tq = A_TQ
    gsum = _head_sum_matrix(A_KV_W)
    return pl.pallas_call(
        _attn_a_kernel,
        grid=(nb, s // tq),
        in_specs=[pl.BlockSpec((1, tq, A_Q_W), lambda b, i: (b, i, 0)),
                  _resident((1, s, A_K_W), lambda b, i: (b, 0, 0)),
                  _resident((1,) + vt.shape[1:], lambda b, i: (b, 0, 0, 0, 0)),
                  _resident(gsum.shape, lambda b, i: (0, 0))],
        out_specs=pl.BlockSpec((1, tq, A_Q_W), lambda b, i: (b, i, 0)),
        out_shape=jax.ShapeDtypeStruct((nb, s, A_Q_W), BF16),
        scratch_shapes=[pltpu.VMEM((A_Q_HEADS, A_K_W, tq), BF16),
                        pltpu.VMEM((A_SCORE_RING, A_TK, tq), F32),
                        pltpu.VMEM((1, A_KV_W), F32)],
        compiler_params=_cparams(2),
        name="attn_a",
    )(qa, ka, vt, gsum)


def _left_lanes(shape):
    return lax.broadcasted_iota(jnp.int32, shape, 1) < HEAD_DIM


def _head_scores(q, kw, hd):
    left = _left_lanes(q.shape)
    zero = jnp.zeros_like(q)
    qh = jnp.where(left, q, zero) if hd == 0 else jnp.where(left, zero, q)
    return _dot_nt(qh, kw)


def _softmax_pv(s, vw):
    m = jnp.max(s, axis=-1, keepdims=True)
    p = jnp.exp(s - m)
    l = jnp.sum(p, axis=-1, keepdims=True)
    o = _dot(p.astype(BF16), vw) * (1.0 / l)
    return o, m + jnp.log(l)


def _run_tasks(tasks, lookahead, score, finish):
    pending = [score(t) for t in tasks[:lookahead]]
    for idx, t in enumerate(tasks):
        if idx + lookahead < len(tasks):
            pending.append(score(tasks[idx + lookahead]))
        finish(t, pending.pop(0))


def _attn_b_kernel(*refs):
    n_g = len(B_GROUPS)
    q_refs, k_refs, v_refs = refs[0:n_g], refs[n_g:2 * n_g], refs[2 * n_g:3 * n_g]
    bias_ref, o_ref = refs[3 * n_g], refs[3 * n_g + 1]
    o_scr, l_scr = refs[3 * n_g + 2:4 * n_g + 2], refs[4 * n_g + 2:5 * n_g + 2]
    tile_i = pl.program_id(2)
    left = _left_lanes((B_TQ, LANES))
    n_sub = B_TILE // B_TQ
    win = B_TQ + 2 * B_HALF

    def body(it, carry):
        geo = {}
        for j in range(B_SUBS_PER_ITER):
            k = it * B_SUBS_PER_ITER + j
            for g, (_, dil) in enumerate(B_GROUPS):
                per_res = n_sub // dil
                shift = per_res.bit_length() - 1
                rho = lax.shift_right_logical(k, jnp.int32(shift))
                sub = k & (per_res - 1)
                u0 = tile_i * (B_TILE // dil) + sub * B_TQ
                ws = jnp.clip(u0 - B_HALF, 0, k_refs[g].shape[2] - win)
                var = 1 - lax.shift_right_arithmetic(ws - (u0 - B_HALF), jnp.int32(B_HALF.bit_length() - 1))
                geo[(j, g)] = (rho, sub, pl.multiple_of(ws, B_HALF), var, dil)

        def score(task):
            j, g, hd = task
            rho, sub, ws, var, _ = geo[(j, g)]
            q = q_refs[g][0, rho, pl.ds(pl.multiple_of(sub * B_TQ, B_TQ), B_TQ), :]
            return _head_scores(q, k_refs[g][0, rho, pl.ds(ws, win), :], hd) + bias_ref[var]

        first = {}

        def finish(task, s):
            j, g, hd = task
            rho, sub, ws, _, dil = geo[(j, g)]
            m = jnp.max(s, axis=-1, keepdims=True)
            p = jnp.exp2(s - m)
            l = jnp.sum(p, axis=-1, keepdims=True)
            o = _dot(p.astype(BF16), v_refs[g][0, rho, pl.ds(ws, win), :]) * (1.0 / l)
            lse = m + jnp.log2(l)
            if hd == 0:
                first[(j, g)] = (o, lse)
                return
            o0, lse0 = first.pop((j, g))
            row0 = rho + (dil * B_TQ) * sub
            rows = pl.ds(row0, B_TQ, stride=dil) if dil > 1 else pl.ds(pl.multiple_of(row0, B_TQ), B_TQ)
            o_scr[g][rows, :] = jnp.where(left, o0, o)
            l_scr[g][rows, :] = jnp.where(left, lse0, lse)

        tasks = [(j, g, hd) for j in range(B_SUBS_PER_ITER) for g in range(n_g) for hd in range(2)]
        _run_tasks(tasks, B_LOOKAHEAD, score, finish)
        return carry

    lax.fori_loop(0, n_sub // B_SUBS_PER_ITER, body, 0)
    lses = [r[...] for r in l_scr]
    m = functools.reduce(jnp.maximum, lses)
    es = [jnp.exp2(l - m) for l in lses]
    num = functools.reduce(lambda a, b: a + b, [e * r[...] for e, r in zip(es, o_scr)])
    o_ref[0] = (num / functools.reduce(lambda a, b: a + b, es)).astype(BF16)


def _attn_b(groups):
    nb, _, s, _ = groups[0][0].shape
    assert s % B_TILE == 0 and all(s // dil >= B_TQ + 2 * B_HALF for _, dil in B_GROUPS)
    assert all(window // (2 * dil) == B_HALF and (B_TILE // B_TQ) % dil == 0 for window, dil in B_GROUPS)
    win = B_TQ + 2 * B_HALF
    shift = jnp.asarray([B_HALF, 0, -B_HALF], jnp.int32)[:, None, None]
    du = (jnp.arange(win, dtype=jnp.int32)[None, None, :] - jnp.arange(B_TQ, dtype=jnp.int32)[None, :, None]
          + shift - B_HALF)
    bias = jnp.where(jnp.abs(du) <= B_HALF, 0.0, NEG_BIG).astype(F32)
    q_specs, kv_specs = [], []
    for _, dil in B_GROUPS:
        q_specs.append(pl.BlockSpec((1, dil, B_TILE // dil, LANES), lambda b, hp, i: (b, 0, i, hp)))
        kv_specs.append(_resident((1, dil, s // dil, LANES), lambda b, hp, i: (b, 0, 0, hp)))
    qs, ks, vs = zip(*groups)
    return pl.pallas_call(
        _attn_b_kernel,
        grid=(nb, B_GROUP_W // LANES, s // B_TILE),
        in_specs=q_specs + kv_specs + kv_specs + [_resident(bias.shape, lambda b, hp, i: (0, 0, 0))],
        out_specs=pl.BlockSpec((1, B_TILE, LANES), lambda b, hp, i: (b, i, hp)),
        out_shape=jax.ShapeDtypeStruct((nb, s, B_GROUP_W), BF16),
        scratch_shapes=[pltpu.VMEM((B_TILE, LANES), F32) for _ in range(2 * len(B_GROUPS))],
        compiler_params=_cparams(3),
        name="attn_b",
    )(*qs, *ks, *vs, bias)


def _finish(y, x, g_post, gate):
    return x + gate * (_rms(y) * g_post)


def _mix_mlp_kernel(*refs, n_parts):
    parts, weights = refs[:n_parts], refs[n_parts:2 * n_parts]
    (x_ref, gmix_ref, gate_mix_ref, gpre_ref, sh_ref, sc_ref, wu_ref, wd_ref, gpost_ref, gate_ref,
     y_ref) = refs[2 * n_parts:]
    y = _dot(parts[0][0], weights[0][...])
    for o_ref, w_ref in zip(parts[1:], weights[1:]):
        y = y + _dot(o_ref[0], w_ref[...])
    x = _finish(y, x_ref[0], gmix_ref[...], gate_mix_ref[0])
    hb = (_rms(x) * gpre_ref[...] * (1.0 + sc_ref[0]) + sh_ref[0]).astype(BF16)
    d = x.shape[-1]
    acc = jnp.zeros(x.shape, F32)
    for j in range(wu_ref.shape[1] // d):
        u = jnp.maximum(_dot(hb, wu_ref[:, j * d:(j + 1) * d]), 0.0)
        acc = acc + _dot((u * u).astype(BF16), wd_ref[j * d:(j + 1) * d, :])
    y_ref[0] = _finish(acc, x, gpost_ref[...], gate_ref[0])


def _mix_mlp(parts, w_outs, x, g_mix, gate_mix, g_pre, shift, scale, w_up, w_down, g_post, gate):
    nb, s, d = x.shape
    tm = TOKEN_TILE
    tok = lambda b, i: (b, i, 0)
    vec = pl.BlockSpec((1, d), lambda b, i: (0, 0))
    mod = pl.BlockSpec((1, 1, d), lambda b, i: (b, 0, 0))
    const = lambda a: _resident(a.shape, lambda b, i: (0, 0))
    return pl.pallas_call(
        functools.partial(_mix_mlp_kernel, n_parts=len(parts)),
        grid=(nb, s // tm),
        in_specs=[pl.BlockSpec((1, tm, p.shape[-1]), tok) for p in parts] + [const(w) for w in w_outs] +
                 [pl.BlockSpec((1, tm, d), tok), vec, mod, vec, mod, mod, const(w_up), const(w_down), vec, mod],
        out_specs=pl.BlockSpec((1, tm, d), tok),
        out_shape=jax.ShapeDtypeStruct(x.shape, F32),
        compiler_params=_cparams(2),
        name="mix_mlp",
    )(*parts, *w_outs, x, g_mix, gate_mix, g_pre, shift, scale, w_up, w_down, g_post, gate)


def _proj1_kernel(x_ref, g_ref, sh_ref, sc_ref, w_ref, q_ref, k_ref, v_ref):
    hb = (_rms(x_ref[0]) * g_ref[...] * (1.0 + sc_ref[0]) + sh_ref[0]).astype(BF16)
    d = q_ref.shape[-1]
    q_ref[0] = (_dot(hb, w_ref[:, 0:d]) * QK_SCALE).astype(BF16)
    k_ref[0] = _dot(hb, w_ref[:, d:2 * d]).astype(BF16)
    v_ref[0] = _dot(hb, w_ref[:, 2 * d:3 * d]).astype(BF16)


def _proj1(x, g_pre, shift, scale, w_in):
    nb, s, d = x.shape
    tm = TOKEN_TILE
    n = w_in.shape[1] // 3
    tok = lambda b, i: (b, i, 0)
    mod = pl.BlockSpec((1, 1, d), lambda b, i: (b, 0, 0))
    out = jax.ShapeDtypeStruct((nb, s, n), BF16)
    return pl.pallas_call(
        _proj1_kernel,
        grid=(nb, s // tm),
        in_specs=[pl.BlockSpec((1, tm, d), tok),
                  pl.BlockSpec((1, d), lambda b, i: (0, 0)), mod, mod,
                  _resident(w_in.shape, lambda b, i: (0, 0))],
        out_specs=(pl.BlockSpec((1, tm, n), tok),) * 3,
        out_shape=(out, out, out),
        compiler_params=_cparams(2),
        name="proj1",
    )(x, g_pre, shift, scale, w_in)


def _attn_c_kernel(q_ref, k_ref, v_ref, bias_ref, o_ref):
    rows = k_ref.shape[1] // GRID_W
    pair_tokens = 2 * GRID_W
    win = C_WIN_ROWS * GRID_W
    blk = pl.program_id(2)
    left = _left_lanes((pair_tokens, LANES))
    starts, variants = [], []
    for j in range(C_PAIRS):
        nominal = 2 * (blk * C_PAIRS + j) - NA_ROWS // 2
        ws = jnp.clip(nominal, 0, rows - C_WIN_ROWS)
        variants.append(2 - lax.shift_right_arithmetic(ws - nominal, 1))
        starts.append(pl.multiple_of(ws * GRID_W, GRID_W))
    toks = lambda j: slice(j * pair_tokens, (j + 1) * pair_tokens)

    def score(task):
        j, hd = task
        return _head_scores(q_ref[0, toks(j), :], k_ref[0, pl.ds(starts[j], win), :], hd) + bias_ref[hd, variants[j]]

    first = {}

    def finish(task, s):
        j, hd = task
        o, _ = _softmax_pv(s, v_ref[0, pl.ds(starts[j], win), :])
        if hd == 0:
            first[j] = o
        else:
            o_ref[0, toks(j), :] = jnp.where(left, first.pop(j), o).astype(BF16)

    _run_tasks([(j, hd) for j in range(C_PAIRS) for hd in range(2)], C_LOOKAHEAD, score, finish)


def _na_bias_table(rpb, rows):
    n_heads, n_dr, n_dc = rpb.shape
    n_pairs = rows // 2
    c = np.arange(GRID_W)
    cs = np.clip(c - NA_COLS // 2, 0, GRID_W - NA_COLS)
    kc = np.arange(GRID_W)
    valid_c = (kc[None, :] >= cs[:, None]) & (kc[None, :] < cs[:, None] + NA_COLS)
    span = 2 * GRID_W - 1
    side = (span - n_dc) // 2
    g = jnp.pad(rpb.astype(F32), ((0, 0), (0, 0), (side, side)), constant_values=NEG_BIG)
    flat = jnp.tile(g, (1, 1, GRID_W))
    toep = flat[:, :, GRID_W - 1:GRID_W - 1 + GRID_W * (span - 1)]
    toep = toep.reshape(n_heads, n_dr, GRID_W, span - 1)[..., :GRID_W]
    toep = jnp.where(valid_c[None, None], toep, NEG_BIG)
    masked = jnp.full((n_heads, GRID_W, GRID_W), NEG_BIG, F32)
    tiles = []
    for i in (0, 1, 2, n_pairs - 2, n_pairs - 1):
        ws = int(np.clip(2 * i - NA_ROWS // 2, 0, rows - C_WIN_ROWS))
        halves = []
        for qr in range(2):
            r = 2 * i + qr
            rs = int(np.clip(r - NA_ROWS // 2, 0, rows - NA_ROWS))
            blocks = []
            for kr in range(ws, ws + C_WIN_ROWS):
                blocks.append(toep[:, kr - r + NA_ROWS - 1] if rs <= kr < rs + NA_ROWS else masked)
            halves.append(jnp.concatenate(blocks, axis=-1))
        tiles.append(jnp.concatenate(halves, axis=1))
    return jnp.stack(tiles, axis=1)


def _attn_c(q, k, v, bias):
    nb, s, w = q.shape
    pairs = w // LANES
    step_tokens = C_PAIRS * 2 * GRID_W
    n_var, tq, win = bias.shape[1:]
    col = lambda b, hp, i: (b, 0, hp)
    return pl.pallas_call(
        _attn_c_kernel,
        grid=(nb, pairs, s // step_tokens),
        in_specs=[pl.BlockSpec((1, step_tokens, LANES), lambda b, hp, i: (b, i, hp)),
                  pl.BlockSpec((1, s, LANES), col),
                  pl.BlockSpec((1, s, LANES), col),
                  _resident((2, n_var, tq, win), lambda b, hp, i: (hp, 0, 0, 0))],
        out_specs=pl.BlockSpec((1, step_tokens, LANES), lambda b, hp, i: (b, i, hp)),
        out_shape=jax.ShapeDtypeStruct((nb, s, w), BF16),
        compiler_params=_cparams(3),
        name="attn_c",
    )(q, k, v, bias)


def _rope_parts(s, tm):
    def cos_sin(pos, dim):
        inv_freq = ROPE_THETA ** (-jnp.arange(0, dim, 2, dtype=F32) / dim)
        ang = pos.astype(F32)[:, None] * inv_freq[None, :]
        return jnp.cos(ang), jnp.sin(ang)

    rows = s // GRID_W
    cr, sr = cos_sin(jnp.arange(rows, dtype=jnp.int32), HEAD_DIM // 2)
    cc, sc = cos_sin(jnp.arange(GRID_W, dtype=jnp.int32), HEAD_DIM // 2)
    zr, zc = jnp.zeros_like(cr), jnp.zeros_like(cc)
    axial = (jnp.concatenate([cr, cr, zr, zr] * 2, axis=-1), jnp.concatenate([-sr, sr, zr, zr] * 2, axis=-1),
             jnp.concatenate([zc, zc, cc, cc] * 2, axis=-1), jnp.concatenate([zc, zc, -sc, sc] * 2, axis=-1))
    c0, s0 = cos_sin(jnp.arange(0, s, tm, dtype=jnp.int32), HEAD_DIM)
    cj, sj = cos_sin(jnp.arange(tm, dtype=jnp.int32), HEAD_DIM)
    wide = lambda a: jnp.concatenate([a] * 4, axis=-1)
    linear = (wide(c0)[:, None, :], wide(s0)[:, None, :], wide(cj), wide(sj))
    return axial + linear


def _head_sum_matrix(n):
    idx = np.arange(n) // HEAD_DIM
    return jnp.asarray(idx[:, None] == idx[None, :], BF16)


def kernel(x, c, ada_w, ada_b, norm_g, ab_w_in, ab_w_out, a_q_gain, a_k_gain,
           c_w_in, c_w_out, c_rpb, mlp_w_up, mlp_w_down):
    nb, s, d = x.shape
    assert s % (C_PAIRS * 2 * GRID_W) == 0 and s // GRID_W >= 2 * C_WIN_ROWS
    mods = _adaln(c, ada_w, ada_b).reshape(ada_w.shape[0], 2, nb, 3, d)

    def mod(layer, which):
        m = mods[layer, which]
        return m[:, 0:1], m[:, 1:2], m[:, 2:3]

    ropes = _rope_parts(s, TOKEN_TILE)
    gain_qk = jnp.concatenate([jnp.tile(a_q_gain[0] * (QK_SCALE * LOG2_E), A_Q_HEADS),
                               jnp.tile(a_k_gain[0], A_KV_HEADS)])[None, :]
    gmat = _head_sum_matrix(A_Q_W + A_KV_W)

    shift, scale, gate = mod(0, 0)
    qa, ka, vt, groups = _proj0(x, norm_g[0, 0:1], shift, scale, ab_w_in[0].astype(BF16), ropes, gain_qk, gmat)
    oa = _attn_a(qa, ka, vt)
    ob = _attn_b(groups)
    w_out = ab_w_out[0].astype(BF16)
    shift, scale, gate_mlp = mod(0, 1)
    x = _mix_mlp([oa, ob], [w_out[:A_Q_W], w_out[A_Q_W:]], x, norm_g[0, 1:2], gate,
                 norm_g[0, 2:3], shift, scale, mlp_w_up[0].astype(BF16), mlp_w_down[0].astype(BF16),
                 norm_g[0, 3:4], gate_mlp)

    shift, scale, gate = mod(1, 0)
    q, k, v = _proj1(x, norm_g[1, 0:1], shift, scale, c_w_in[0].astype(BF16))
    oc = _attn_c(q, k, v, _na_bias_table(c_rpb[0], s // GRID_W))
    shift, scale, gate_mlp = mod(1, 1)
    return _mix_mlp([oc], [c_w_out[0].astype(BF16)], x, norm_g[1, 1:2], gate,
                    norm_g[1, 2:3], shift, scale, mlp_w_up[1].astype(BF16), mlp_w_down[1].astype(BF16),
                    norm_g[1, 3:4], gate_mlp)
```
